```python
import numpy as np
import jax
import jax.numpy as jnp
from jax import lax

D_MODEL = 1024
BATCH = 8
SEQ = 4096
DEPTH = 2

HEAD_DIM = 64
N_MIX_HEADS = D_MODEL // HEAD_DIM
MIX_WIDTH = N_MIX_HEADS * HEAD_DIM
NSA_HEADS = N_MIX_HEADS // 2
FOX_HEADS = N_MIX_HEADS // 4
SB_HEADS = N_MIX_HEADS - NSA_HEADS - FOX_HEADS
NSA_KV_GROUPS = 2
NSA_HEADS_PER_GROUP = NSA_HEADS // NSA_KV_GROUPS
NSA_CMP_LEN = 32
NSA_CMP_STRIDE = 16
NSA_CMP_HIDDEN = 128
NSA_SLC_LEN = 64
NSA_TOP_N = 16
NSA_WINDOW = 512
NSA_N_BRANCHES = 3
NSA_Q_BLOCK = 32
Q_BLOCK = 128
ROPE_THETA = 10000.0
RMS_EPS = 1e-6
PLE_DIM = 256
D_FF = -(-(8 * D_MODEL) // (3 * 256)) * 256
NEG_INF = -1e30
FORCE = 1e9
TINY = 1e-20

_NSA_KV = NSA_KV_GROUPS * HEAD_DIM
IN_SPLITS = (
    NSA_HEADS * HEAD_DIM,
    _NSA_KV, _NSA_KV,
    _NSA_KV, _NSA_KV,
    _NSA_KV, _NSA_KV,
    NSA_HEADS * NSA_N_BRANCHES,
    FOX_HEADS * HEAD_DIM, FOX_HEADS * HEAD_DIM, FOX_HEADS * HEAD_DIM,
    FOX_HEADS,
    SB_HEADS * HEAD_DIM, SB_HEADS * HEAD_DIM, SB_HEADS * HEAD_DIM,
)
IN_WIDTH = sum(IN_SPLITS)

kernel_name = "hybrid_nsa_fox_stickbreaking_parallel_heads"


def rms_norm(x, g):
    xf = x.astype(jnp.float32)
    y = xf * lax.rsqrt(jnp.mean(xf * xf, axis=-1, keepdims=True) + RMS_EPS)
    return (y * g.astype(jnp.float32)).astype(x.dtype)


def rope(x, positions):
    half = x.shape[-1] // 2
    inv_freq = ROPE_THETA ** (-jnp.arange(half, dtype=jnp.float32) / half)
    ang = positions.astype(jnp.float32)[..., None] * inv_freq
    cos = jnp.cos(ang)[:, :, None, :]
    sin = jnp.sin(ang)[:, :, None, :]
    xf = x.astype(jnp.float32)
    x1, x2 = xf[..., :half], xf[..., half:]
    return jnp.concatenate([x1 * cos - x2 * sin, x2 * cos + x1 * sin], axis=-1).astype(x.dtype)


def masked_softmax(logits, mask):
    l = jnp.where(mask, logits, NEG_INF)
    m = jnp.max(l, axis=-1, keepdims=True)
    e = jnp.where(mask, jnp.exp(l - m), 0.0)
    return e / jnp.maximum(jnp.sum(e, axis=-1, keepdims=True), TINY)


def compress_blocks(kv, pos_emb, w1, w2):
    b, s, g, d = kv.shape
    n_cmp = (s - NSA_CMP_LEN) // NSA_CMP_STRIDE + 1
    idx = np.arange(n_cmp)[:, None] * NSA_CMP_STRIDE + np.arange(NSA_CMP_LEN)[None, :]
    blocks = kv[:, idx] + pos_emb[:, None, :]
    blocks = blocks.transpose(0, 1, 3, 2, 4).reshape(b, n_cmp, g, NSA_CMP_LEN * d)
    return jax.nn.gelu(blocks @ w1) @ w2


def nsa_attention(q, k_cmp, v_cmp, k_slc, v_slc, k_win, v_win, gates,
                  pos_k, w1_k, w2_k, pos_v, w1_v, w2_v):
    b, s, h, d = q.shape
    g_, hpg, tq, win = NSA_KV_GROUPS, NSA_HEADS_PER_GROUP, NSA_Q_BLOCK, NSA_WINDOW
    scale = d ** -0.5
    kc = compress_blocks(k_cmp, pos_k, w1_k, w2_k)
    vc = compress_blocks(v_cmp, pos_v, w1_v, w2_v)
    n_cmp = kc.shape[1]
    cmp_start = np.arange(n_cmp) * NSA_CMP_STRIDE
    cmp_end = jnp.asarray(cmp_start + NSA_CMP_LEN - 1)
    n_blk = s // NSA_SLC_LEN
    n_sel = min(NSA_TOP_N, n_blk)
    blk_start = np.arange(n_blk) * NSA_SLC_LEN
    overlap = jnp.asarray(((cmp_start[:, None] < blk_start[None, :] + NSA_SLC_LEN)
                           & (cmp_start[:, None] + NSA_CMP_LEN > blk_start[None, :])).astype(np.float32))
    ks_blocks = k_slc.reshape(b, n_blk, NSA_SLC_LEN, g_, d).transpose(0, 3, 1, 2, 4)
    vs_blocks = v_slc.reshape(b, n_blk, NSA_SLC_LEN, g_, d).transpose(0, 3, 1, 2, 4)
    pad = ((0, 0), (win, 0), (0, 0), (0, 0))
    kw_pad = jnp.pad(k_win, pad)
    vw_pad = jnp.pad(v_win, pad)
    b_idx = jnp.arange(b)[:, None, None, None]
    g_idx = jnp.arange(g_)[None, :, None, None]
    blk_ids = jnp.arange(n_blk)
    win_off = jnp.arange(win + tq)
    slc_off = jnp.arange(NSA_SLC_LEN)

    def step(args):
        q_blk, g_blk, q0 = args
        t = q0 + jnp.arange(tq)
        qg = q_blk.reshape(b, tq, g_, hpg, d)
        s_c = jnp.einsum('btghd,bngd->bghtn', qg, kc).astype(jnp.float32) * scale
        p_c = masked_softmax(s_c, cmp_end[None, :] <= t[:, None])
        o_c = jnp.einsum('bghtn,bngd->btghd', p_c.astype(vc.dtype), vc)
        imp = jnp.einsum('bghtn,nj->bgtj', p_c, overlap)
        cur = t // NSA_SLC_LEN
        valid = blk_ids[None, :] <= cur[:, None]
        forced = ((blk_ids[None, :] == 0) | (blk_ids[None, :] == cur[:, None])
                  | (blk_ids[None, :] == cur[:, None] - 1))
        score = jnp.where(forced, FORCE, jnp.where(valid, imp, -FORCE))
        _, sel = lax.top_k(score, n_sel)
        ks = ks_blocks[b_idx, g_idx, sel]
        vs = vs_blocks[b_idx, g_idx, sel]
        s_s = jnp.einsum('btghd,bgtnkd->bghtnk', qg, ks).astype(jnp.float32) * scale
        kpos = sel[..., None] * NSA_SLC_LEN + slc_off
        mask_s = (kpos <= t[:, None, None])[:, :, None].reshape(b, g_, 1, tq, n_sel * NSA_SLC_LEN)
        p_s = masked_softmax(s_s.reshape(b, g_, hpg, tq, n_sel * NSA_SLC_LEN), mask_s)
        p_s = p_s.reshape(b, g_, hpg, tq, n_sel, NSA_SLC_LEN)
        o_s = jnp.einsum('bghtnk,bgtnkd->btghd', p_s.astype(vs.dtype), vs)
        kw = lax.dynamic_slice_in_dim(kw_pad, q0, win + tq, axis=1)
        vw = lax.dynamic_slice_in_dim(vw_pad, q0, win + tq, axis=1)
        s_pos = q0 - win + win_off
        diff = t[:, None] - s_pos[None, :]
        mask_w = (diff >= 0) & (diff < win) & (s_pos[None, :] >= 0)
        s_w = jnp.einsum('btghd,bkgd->bghtk', qg, kw).astype(jnp.float32) * scale
        p_w = masked_softmax(s_w, mask_w)
        o_w = jnp.einsum('bghtk,bkgd->btghd', p_w.astype(vw.dtype), vw)
        gt = g_blk.reshape(b, tq, g_, hpg, NSA_N_BRANCHES)
        o = gt[..., 0:1] * o_c + gt[..., 1:2] * o_s + gt[..., 2:3] * o_w
        return o.reshape(b, tq, h, d)

    nqb = s // tq
    q_b = q.reshape(b, nqb, tq, h, d).swapaxes(0, 1)
    g_b = gates.reshape(b, nqb, tq, h, NSA_N_BRANCHES).swapaxes(0, 1)
    starts = jnp.arange(nqb, dtype=jnp.int32) * tq
    out = lax.map(step, (q_b, g_b, starts))
    return out.swapaxes(0, 1).reshape(b, s, h, d)


def forgetting_attention(q, k, v, log_f):
    b, s, h, d = q.shape
    scale = d ** -0.5
    c = jnp.cumsum(log_f.astype(jnp.float32), axis=1).transpose(0, 2, 1)
    s_idx = jnp.arange(s)

    def step(args):
        q_blk, c_blk, q0 = args
        t = q0 + jnp.arange(Q_BLOCK)
        logits = (jnp.einsum('bthd,bshd->bhts', q_blk, k).astype(jnp.float32) * scale
                  + c_blk[..., None] - c[:, :, None, :])
        p = masked_softmax(logits, s_idx[None, :] <= t[:, None])
        return jnp.einsum('bhts,bshd->bthd', p.astype(v.dtype), v)

    nqb = s // Q_BLOCK
    q_b = q.reshape(b, nqb, Q_BLOCK, h, d).swapaxes(0, 1)
    c_b = c.reshape(b, h, nqb, Q_BLOCK).transpose(2, 0, 1, 3)
    starts = jnp.arange(nqb, dtype=jnp.int32) * Q_BLOCK
    out = lax.map(step, (q_b, c_b, starts))
    return out.swapaxes(0, 1).reshape(b, s, h, d)


def stick_breaking_attention(q, k, v):
    b, s, h, d = q.shape
    scale = d ** -0.5
    s_idx = jnp.arange(s)

    def step(args):
        q_blk, q0 = args
        t = q0 + jnp.arange(Q_BLOCK)
        z = jnp.einsum('bthd,bshd->bhts', q_blk, k).astype(jnp.float32) * scale
        mask = s_idx[None, :] < t[:, None]
        log_rest = jnp.where(mask, jax.nn.log_sigmoid(-z), 0.0)
        after = lax.cumsum(log_rest, axis=3, reverse=True) - log_rest
        a = jnp.where(mask, jnp.exp(jax.nn.log_sigmoid(z) + after), 0.0)
        return jnp.einsum('bhts,bshd->bthd', a.astype(v.dtype), v)

    nqb = s // Q_BLOCK
    q_b = q.reshape(b, nqb, Q_BLOCK, h, d).swapaxes(0, 1)
    starts = jnp.arange(nqb, dtype=jnp.int32) * Q_BLOCK
    out = lax.map(step, (q_b, starts))
    return out.swapaxes(0, 1).reshape(b, s, h, d)


def setup_inputs(seed: int = 0) -> dict:
    key = jax.random.key(seed)
    ks = jax.random.split(key, 24)
    f32 = jnp.float32
    L = DEPTH

    def nrm(k, shape, scale):
        return jax.random.normal(k, shape, f32) * scale

    cmp_in = NSA_CMP_LEN * HEAD_DIM
    return {
        "x": nrm(ks[0], (BATCH, SEQ, D_MODEL), 1.0),
        "p": nrm(ks[1], (DEPTH, BATCH, SEQ, PLE_DIM), 1.0),
        "positions": jnp.tile(jnp.arange(SEQ, dtype=jnp.int32)[None, :], (BATCH, 1)),
        "norm_mix": 1.0 + nrm(ks[2], (L, D_MODEL), 0.02),
        "w_in": nrm(ks[3], (L, D_MODEL, IN_WIDTH), D_MODEL ** -0.5),
        "b_nsa_gate": nrm(ks[4], (L, NSA_HEADS * NSA_N_BRANCHES), 0.1),
        "b_forget": 2.0 + nrm(ks[5], (L, FOX_HEADS), 0.5),
        "nsa_cmp_pos_k": nrm(ks[6], (L, NSA_CMP_LEN, HEAD_DIM), 0.1),
        "nsa_cmp_w1_k": nrm(ks[7], (L, cmp_in, NSA_CMP_HIDDEN), cmp_in ** -0.5),
        "nsa_cmp_w2_k": nrm(ks[8], (L, NSA_CMP_HIDDEN, HEAD_DIM), NSA_CMP_HIDDEN ** -0.5),
        "nsa_cmp_pos_v": nrm(ks[9], (L, NSA_CMP_LEN, HEAD_DIM), 0.1),
        "nsa_cmp_w1_v": nrm(ks[10], (L, cmp_in, NSA_CMP_HIDDEN), cmp_in ** -0.5),
        "nsa_cmp_w2_v": nrm(ks[11], (L, NSA_CMP_HIDDEN, HEAD_DIM), NSA_CMP_HIDDEN ** -0.5),
        "head_norm": 1.0 + nrm(ks[12], (L, MIX_WIDTH), 0.02),
        "w_out": nrm(ks[13], (L, MIX_WIDTH, D_MODEL), MIX_WIDTH ** -0.5),
        "norm_ffn": 1.0 + nrm(ks[14], (L, D_MODEL), 0.02),
        "w_ffn_gate": nrm(ks[15], (L, D_MODEL, D_FF), D_MODEL ** -0.5),
        "w_ffn_up": nrm(ks[16], (L, D_MODEL, D_FF), D_MODEL ** -0.5),
        "w_ffn_down": nrm(ks[17], (L, D_FF, D_MODEL), D_FF ** -0.5),
        "norm_ple": 1.0 + nrm(ks[18], (L, D_MODEL), 0.02),
        "w_ple_proj": nrm(ks[19], (L, PLE_DIM, D_MODEL), PLE_DIM ** -0.5),
        "w_ple_gate": nrm(ks[20], (L, D_MODEL, D_MODEL), D_MODEL ** -0.5),
        "norm_final": 1.0 + nrm(ks[21], (D_MODEL,), 0.02),
    }


def reference(x, p, positions, norm_mix, w_in, b_nsa_gate, b_forget,
              nsa_cmp_pos_k, nsa_cmp_w1_k, nsa_cmp_w2_k,
              nsa_cmp_pos_v, nsa_cmp_w1_v, nsa_cmp_w2_v,
              head_norm, w_out, norm_ffn, w_ffn_gate, w_ffn_up, w_ffn_down,
              norm_ple, w_ple_proj, w_ple_gate, norm_final):
    b, s, _ = x.shape
    offsets = [int(o) for o in np.cumsum(IN_SPLITS)[:-1]]

    def heads(a, n):
        return a.reshape(b, s, n, HEAD_DIM)

    h = x
    for i in range(DEPTH):
        hn = rms_norm(h, norm_mix[i])
        proj = hn @ w_in[i]
        (nq, nkc, nvc, nks, nvs, nkw, nvw, ngate,
         fq, fk, fv, ff, sq, sk, sv) = jnp.split(proj, offsets, axis=-1)
        nq = rope(heads(nq, NSA_HEADS), positions)
        nkc = rope(heads(nkc, NSA_KV_GROUPS), positions)
        nks = rope(heads(nks, NSA_KV_GROUPS), positions)
        nkw = rope(heads(nkw, NSA_KV_GROUPS), positions)
        gates = jax.nn.sigmoid(ngate.reshape(b, s, NSA_HEADS, NSA_N_BRANCHES)
                               + b_nsa_gate[i].reshape(NSA_HEADS, NSA_N_BRANCHES))
        o_nsa = nsa_attention(nq, nkc, heads(nvc, NSA_KV_GROUPS), nks, heads(nvs, NSA_KV_GROUPS),
                              nkw, heads(nvw, NSA_KV_GROUPS), gates,
                              nsa_cmp_pos_k[i], nsa_cmp_w1_k[i], nsa_cmp_w2_k[i],
                              nsa_cmp_pos_v[i], nsa_cmp_w1_v[i], nsa_cmp_w2_v[i])
        log_f = jax.nn.log_sigmoid((ff + b_forget[i]).astype(jnp.float32))
        o_fox = forgetting_attention(heads(fq, FOX_HEADS), heads(fk, FOX_HEADS),
                                     heads(fv, FOX_HEADS), log_f)
        o_sb = stick_breaking_attention(heads(sq, SB_HEADS), heads(sk, SB_HEADS), heads(sv, SB_HEADS))
        o = jnp.concatenate([o_nsa, o_fox, o_sb], axis=2)
        o = rms_norm(o, head_norm[i].reshape(N_MIX_HEADS, HEAD_DIM))
        h = h + o.reshape(b, s, MIX_WIDTH) @ w_out[i]
        hn = rms_norm(h, norm_ffn[i])
        h = h + (jax.nn.silu(hn @ w_ffn_gate[i]) * (hn @ w_ffn_up[i])) @ w_ffn_down[i]
        gate = jax.nn.sigmoid(rms_norm(h, norm_ple[i]) @ w_ple_gate[i])
        h = h + (p[i] @ w_ple_proj[i]) * gate
    return rms_norm(h, norm_final)
```

```python
import functools

import numpy as np
import jax
import jax.numpy as jnp
from jax import lax
from jax.experimental import pallas as pl
from jax.experimental.pallas import tpu as pltpu

F32 = jnp.float32
BF16 = jnp.bfloat16

HEAD_DIM = 64
NSA_HEADS = 8
NSA_GROUPS = 2
NSA_HPG = 4
FOX_HEADS = 4
SB_HEADS = 4
N_HEADS = 16
CMP_LEN = 32
CMP_STRIDE = 16
CMP_HIDDEN = 128
SLC_LEN = 64
TOP_N = 16
WINDOW = 512
ROPE_THETA = 10000.0
RMS_EPS = 1e-6
NEG_INF = -1e30
FORCE = 1e9
TINY = 1e-20
Q_SCALE = HEAD_DIM ** -0.5

LANES = 128
IN_PAD = 3072
VMEM_LIMIT = 56 * 1024 * 1024

_OFF = dict(nq=0, nkc=512, nvc=640, nks=768, nvs=896, nkw=1024, nvw=1152, ngate=1280,
            fq=1304, fk=1560, fv=1816, ff=2072, sq=2076, sk=2332, sv=2588)


def _dot(a, b):
    return jnp.dot(a, b, preferred_element_type=F32)


def _dot_nt(a, b):
    return lax.dot_general(a, b, (((1,), (1,)), ((), ())), preferred_element_type=F32)


def _cparams(sem):
    return pltpu.CompilerParams(dimension_semantics=sem, vmem_limit_bytes=VMEM_LIMIT)


def _const_spec(shape):
    nd = len(shape)
    return pl.BlockSpec(shape, lambda *_: (0,) * nd, pipeline_mode=pl.Buffered(1))


def _permute_w_in(w):
    d = w.shape[0]

    def cols(name, width):
        return w[:, _OFF[name]:_OFF[name] + width]

    gate = cols("ngate", 24)
    zeros = lambda n: jnp.zeros((d, n), w.dtype)
    parts = [cols("nq", 512), cols("nkc", 128), cols("nks", 128), cols("nkw", 128),
             cols("nvc", 128), cols("nvs", 128), cols("nvw", 128),
             cols("fq", 256), cols("fk", 256), cols("fv", 256),
             cols("sq", 256), cols("sk", 256), cols("sv", 256),
             gate[:, :12], cols("ff", 4), zeros(112), gate[:, 12:], zeros(116)]
    return jnp.concatenate(parts, axis=1)


def _misc_bias(b_gate, b_forget):
    z = lambda n: jnp.zeros((n,), F32)
    return jnp.concatenate([b_gate[:12], b_forget, z(112), b_gate[12:], z(116)]).reshape(1, 256)


def _rope_half(x, cos, sin_signed):
    lane = lax.broadcasted_iota(jnp.int32, x.shape, 1)
    first = (lane % HEAD_DIM) < (HEAD_DIM // 2)
    partner = jnp.where(first, pltpu.roll(x, LANES - 32, 1), pltpu.roll(x, 32, 1))
    return x * cos + partner * sin_signed


def _log_sigmoid(x):
    return jnp.minimum(x, 0.0) - jnp.log(1.0 + jnp.exp(-jnp.abs(x)))


def _inproj_kernel(x_ref, g_ref, w_ref, cos_ref, sin_ref, bias_ref,
                   qn_ref, kvn_ref, cmp_ref, fox_ref, sb_ref, misc_ref):
    x = x_ref[...]
    ms = jnp.mean(x * x, axis=-1, keepdims=True)
    hn = (x * lax.rsqrt(ms + RMS_EPS) * g_ref[...]).astype(BF16)
    cos = cos_ref[...]
    sin = sin_ref[...]

    plan = [(qn_ref, 0, True, True), (qn_ref, 2, True, True),
            (qn_ref, 4, True, True), (qn_ref, 6, True, True),
            (cmp_ref, 0, True, False), (kvn_ref, 0, True, False),
            (kvn_ref, 2, True, False), (cmp_ref, 2, False, False),
            (kvn_ref, 4, False, False), (kvn_ref, 6, False, False),
            (fox_ref, 0, False, True), (fox_ref, 2, False, True),
            (fox_ref, 4, False, False), (fox_ref, 6, False, False),
            (fox_ref, 8, False, False), (fox_ref, 10, False, False),
            (sb_ref, 0, False, True), (sb_ref, 2, False, True),
            (sb_ref, 4, False, False), (sb_ref, 6, False, False),
            (sb_ref, 8, False, False), (sb_ref, 10, False, False)]
    for c in range(IN_PAD // 256):
        res = _dot(hn, w_ref[:, c * 256:(c + 1) * 256])
        if c == IN_PAD // 256 - 1:
            z = res + bias_ref[...]
            lane = lax.broadcasted_iota(jnp.int32, z.shape, 1)
            is_forget = (lane >= 12) & (lane < 16)
            misc_ref[...] = jnp.where(is_forget, _log_sigmoid(z), jax.nn.sigmoid(z))
            continue
        for half in range(2):
            dst, slot, rot, scl = plan[2 * c + half]
            y = res[:, half * LANES:(half + 1) * LANES]
            if rot:
                y = _rope_half(y, cos, sin)
            if scl:
                y = y * Q_SCALE
            y = y.astype(dst.dtype)
            dst[slot] = y[:, :HEAD_DIM]
            dst[slot + 1] = y[:, HEAD_DIM:]


def _in_projection(h2d, gain, w_perm, cos_t, sin_t, bias, tm):
    t, d = h2d.shape
    grid = (t // tm,)
    hm = lambda n, dt: jax.ShapeDtypeStruct((n, t, HEAD_DIM), dt)
    hm_spec = lambda n: pl.BlockSpec((n, tm, HEAD_DIM), lambda i: (0, i, 0))
    return pl.pallas_call(
        _inproj_kernel,
        grid=grid,
        in_specs=[pl.BlockSpec((tm, d), lambda i: (i, 0)),
                  _const_spec((1, d)),
                  _const_spec((d, IN_PAD)),
                  pl.BlockSpec((tm, LANES), lambda i: (i, 0)),
                  pl.BlockSpec((tm, LANES), lambda i: (i, 0)),
                  _const_spec((1, 256))],
        out_specs=[hm_spec(8), hm_spec(8), hm_spec(4), hm_spec(12), hm_spec(12),
                   pl.BlockSpec((tm, 256), lambda i: (i, 0))],
        out_shape=[hm(8, BF16), hm(8, BF16), hm(4, F32), hm(12, BF16), hm(12, BF16),
                   jax.ShapeDtypeStruct((t, 256), F32)],
        compiler_params=_cparams(("parallel",)),
        name="in_projection",
    )(h2d, gain, w_perm, cos_t, sin_t, bias)


def _gelu_tanh(x):
    c = np.float32(np.sqrt(2.0 / np.pi))
    return 0.5 * x * (1.0 + jnp.tanh(c * (x + 0.044715 * (x * x * x))))


def _compress_kernel(ch_ref, pos_ref, w1_ref, w2_ref, out_ref):
    ch = ch_ref[0, 0]
    half = CMP_STRIDE * HEAD_DIM
    top = _dot((ch + pos_ref[0, 0:1, :]).astype(BF16), w1_ref[0, :half, :])
    bot = _dot((ch + pos_ref[0, 1:2, :]).astype(BF16), w1_ref[0, half:, :])
    n = ch.shape[0]
    hidden = top + pltpu.roll(bot, n - 1, 0)
    act = _gelu_tanh(hidden).astype(BF16)
    out_ref[0, 0] = _dot(act, w2_ref[0]).astype(out_ref.dtype)


def _compress(cmp4, pos, w1, w2, b, s):
    n_chunk = s // CMP_STRIDE
    width = CMP_STRIDE * HEAD_DIM
    chunks = cmp4.reshape(4, b, n_chunk, width)
    return pl.pallas_call(
        _compress_kernel,
        grid=(4, b),
        in_specs=[pl.BlockSpec((1, 1, n_chunk, width), lambda w, i: (w, i, 0, 0)),
                  pl.BlockSpec((1, 2, width), lambda w, i: (w // 2, 0, 0)),
                  pl.BlockSpec((1, 2 * width, CMP_HIDDEN), lambda w, i: (w // 2, 0, 0)),
                  pl.BlockSpec((1, CMP_HIDDEN, HEAD_DIM), lambda w, i: (w // 2, 0, 0))],
        out_specs=pl.BlockSpec((1, 1, n_chunk, HEAD_DIM), lambda w, i: (w, i, 0, 0)),
        out_shape=jax.ShapeDtypeStruct((4, b, n_chunk, HEAD_DIM), BF16),
        compiler_params=_cparams(("parallel", "parallel")),
        name="nsa_compress",
    )(chunks, pos, w1, w2)


def _split3(x):
    h1 = x.astype(BF16)
    r1 = x - h1.astype(F32)
    h2 = r1.astype(BF16)
    h3 = (r1 - h2.astype(F32)).astype(BF16)
    return h1, h2, h3


def _cumsum_kernel(lf_ref, out_ref):
    nh, r, _ = lf_ref.shape[1:]
    row = lax.broadcasted_iota(jnp.int32, (LANES, LANES), 0)
    col = lax.broadcasted_iota(jnp.int32, (LANES, LANES), 1)
    upper = jnp.where(row <= col, 1.0, 0.0).astype(BF16)
    rr = lax.broadcasted_iota(jnp.int32, (r, r), 0)
    rc = lax.broadcasted_iota(jnp.int32, (r, r), 1)
    strict = jnp.where(rc < rr, 1.0, 0.0).astype(BF16)
    for h in range(nh):
        x = lf_ref[0, h]
        within = sum(_dot(part, upper) for part in _split3(x))
        tot = jnp.broadcast_to(within[:, LANES - 1:LANES], (r, LANES))
        before = sum(_dot(strict, part) for part in _split3(tot))
        out_ref[0, h] = -(within + before)


def _neg_cumsum(lf):
    b, nh, r, _ = lf.shape
    spec = pl.BlockSpec((1, nh, r, LANES), lambda i: (i, 0, 0, 0))
    return pl.pallas_call(
        _cumsum_kernel, grid=(b,), in_specs=[spec], out_specs=spec,
        out_shape=jax.ShapeDtypeStruct(lf.shape, F32),
        compiler_params=_cparams(("parallel",)),
        name="forget_cumsum",
    )(lf)


def _nsa_cmp_kernel(q_ref, kc_ref, vc_ref, ov_ref, oc_ref, selb_ref, *, tq, n_sel):
    q0 = pl.program_id(2) * tq
    q4 = q_ref[:, 0].reshape(NSA_HPG * tq, HEAD_DIM)
    kc = kc_ref[0, 0]
    vc = vc_ref[0, 0]
    n_pad = kc.shape[0]
    n_blk = ov_ref.shape[1]
    s = _dot_nt(q4, kc).reshape(NSA_HPG, tq, n_pad)
    t = q0 + lax.broadcasted_iota(jnp.int32, (1, tq, 1), 1)
    n = lax.broadcasted_iota(jnp.int32, (1, 1, n_pad), 2)
    mask = (n * CMP_STRIDE + (CMP_LEN - 1) <= t) & (n < n_pad - 1)
    l = jnp.where(mask, s, NEG_INF)
    m = jnp.max(l, axis=-1, keepdims=True)
    e = jnp.where(mask, jnp.exp(l - m), 0.0)
    p = e / jnp.maximum(jnp.sum(e, axis=-1, keepdims=True), TINY)
    pb = p.astype(BF16).reshape(NSA_HPG * tq, n_pad)
    oc_ref[:, 0] = _dot(pb, vc).reshape(NSA_HPG, tq, HEAD_DIM)
    imp = jnp.sum(_dot(pb, ov_ref[...]).reshape(NSA_HPG, tq, n_blk), axis=0)

    t2 = q0 + lax.broadcasted_iota(jnp.int32, (tq, 1), 0)
    blk = lax.broadcasted_iota(jnp.int32, (tq, n_blk), 1)
    cur = t2 // SLC_LEN
    forced = (blk == 0) | (blk == cur) | (blk == cur - 1)
    score = jnp.where(forced, FORCE, jnp.where(blk <= cur, imp, -FORCE))
    blk_f = blk.astype(F32)
    sel = jnp.zeros((tq, n_blk), F32)
    for _ in range(n_sel):
        best = jnp.max(score, axis=-1, keepdims=True)
        first = jnp.min(jnp.where(score == best, blk_f, float(n_blk)), axis=-1, keepdims=True)
        pick = blk_f == first
        sel = jnp.where(pick, 1.0, sel)
        score = jnp.where(pick, -jnp.inf, score)
    selb_ref[0, 0] = jnp.where(sel > 0.0, 0.0, NEG_INF).astype(selb_ref.dtype)


def _nsa_cmp(qn, comp, overlap, b, s, tq):
    n_pad = comp.shape[2]
    n_blk = overlap.shape[1]
    grid = (b, NSA_GROUPS, s // tq)
    kern = functools.partial(_nsa_cmp_kernel, tq=tq, n_sel=min(TOP_N, n_blk))
    return pl.pallas_call(
        kern, grid=grid,
        in_specs=[pl.BlockSpec((NSA_HPG, 1, tq, HEAD_DIM), lambda i, g, j: (g, i, j, 0)),
                  pl.BlockSpec((1, 1, n_pad, HEAD_DIM), lambda i, g, j: (g, i, 0, 0)),
                  pl.BlockSpec((1, 1, n_pad, HEAD_DIM), lambda i, g, j: (2 + g, i, 0, 0)),
                  _const_spec((n_pad, n_blk))],
        out_specs=[pl.BlockSpec((NSA_HPG, 1, tq, HEAD_DIM), lambda i, g, j: (g, i, j, 0)),
                   pl.BlockSpec((1, 1, tq, n_blk), lambda i, g, j: (i, g, j, 0))],
        out_shape=[jax.ShapeDtypeStruct((NSA_HEADS, b, s, HEAD_DIM), F32),
                   jax.ShapeDtypeStruct((b, NSA_GROUPS, s, n_blk), BF16)],
        compiler_params=_cparams(("parallel", "parallel", "parallel")),
        name="nsa_compressed_select",
    )(qn, comp, comp, overlap)


def _head_rms(o, gain):
    return o * lax.rsqrt(jnp.mean(o * o, axis=-1, keepdims=True) + RMS_EPS) * gain


def _nsa_main_kernel(q_ref, ks_ref, kw_ref, vs_ref, vw_ref, selb_ref, oc_ref, gate_ref, gain_ref,
                     out_ref, *, tq, tk):
    q0 = pl.program_id(2) * tq
    hq = NSA_HPG * tq
    q4 = q_ref[:, 0].reshape(hq, HEAD_DIM)
    selb = selb_ref[0, 0]
    n_blk = selb.shape[1]
    blk_row = lax.broadcasted_iota(jnp.int32, (n_blk, tk), 0)
    col = lax.broadcasted_iota(jnp.int32, (n_blk, tk), 1)
    t = q0 + lax.broadcasted_iota(jnp.int32, (1, tq, 1), 1)

    def slc_tile(j, carry, causal):
        m, l, acc = carry
        start = pl.multiple_of(j * tk, tk)
        k = ks_ref[0, 0, pl.ds(start, tk), :]
        v = vs_ref[0, 0, pl.ds(start, tk), :]
        expand = jnp.where((start + col) // SLC_LEN == blk_row, 1.0, 0.0).astype(BF16)
        bias = _dot(selb, expand)
        sc = _dot_nt(q4, k).reshape(NSA_HPG, tq, tk) + bias[None]
        if causal:
            kpos = start + lax.broadcasted_iota(jnp.int32, (1, 1, tk), 2)
            sc = jnp.where(kpos <= t, sc, NEG_INF)
        m_new = jnp.maximum(m, jnp.max(sc, axis=-1, keepdims=True))
        alpha = jnp.exp(m - m_new)
        p = jnp.exp(sc - m_new)
        l = alpha * l + jnp.sum(p, axis=-1, keepdims=True)
        pv = _dot(p.astype(BF16).reshape(hq, tk), v).reshape(NSA_HPG, tq, HEAD_DIM)
        return m_new, l, alpha * acc + pv

    init = (jnp.full((NSA_HPG, tq, 1), NEG_INF, F32), jnp.zeros((NSA_HPG, tq, 1), F32),
            jnp.zeros((NSA_HPG, tq, HEAD_DIM), F32))
    n_full = q0 // tk
    carry = lax.fori_loop(0, n_full, lambda j, c: slc_tile(j, c, False), init)
    _, l_s, acc_s = slc_tile(n_full, carry, True)
    o_s = acc_s / l_s

    span = WINDOW + tq
    w0 = pl.multiple_of(jnp.maximum(q0 - WINDOW, 0), tq)
    kwin = kw_ref[0, 0, pl.ds(w0, span), :]
    vwin = vw_ref[0, 0, pl.ds(w0, span), :]
    sw = _dot_nt(q4, kwin).reshape(NSA_HPG, tq, span)
    kpos = w0 + lax.broadcasted_iota(jnp.int32, (1, 1, span), 2)
    mask = (kpos <= t) & (t - kpos < WINDOW)
    sw = jnp.where(mask, sw, NEG_INF)
    mw = jnp.max(sw, axis=-1, keepdims=True)
    ew = jnp.where(mask, jnp.exp(sw - mw), 0.0)
    lw = jnp.maximum(jnp.sum(ew, axis=-1, keepdims=True), TINY)
    o_w = _dot(ew.astype(BF16).reshape(hq, span), vwin).reshape(NSA_HPG, tq, HEAD_DIM) / lw

    gates = gate_ref[...]
    heads = []
    for hh in range(NSA_HPG):
        g0 = gates[:, 3 * hh:3 * hh + 1]
        g1 = gates[:, 3 * hh + 1:3 * hh + 2]
        g2 = gates[:, 3 * hh + 2:3 * hh + 3]
        o = g0 * oc_ref[hh, 0] + g1 * o_s[hh] + g2 * o_w[hh]
        heads.append(_head_rms(o, gain_ref[hh]).astype(out_ref.dtype))
    out_ref[...] = jnp.concatenate(heads, axis=-1)


def _nsa_main(qn, kvn, selb, oc, misc, gain16, b, s, tq, tk):
    nq = s // tq
    n_blk = selb.shape[-1]
    grid = (b, NSA_GROUPS, nq)
    kv_spec = lambda base: pl.BlockSpec((1, 1, s, HEAD_DIM), lambda i, g, j: (base + g, i, 0, 0))
    kern = functools.partial(_nsa_main_kernel, tq=tq, tk=tk)
    return pl.pallas_call(
        kern, grid=grid,
        in_specs=[pl.BlockSpec((NSA_HPG, 1, tq, HEAD_DIM), lambda i, g, j: (g, i, j, 0)),
                  kv_spec(0), kv_spec(2), kv_spec(4), kv_spec(6),
                  pl.BlockSpec((1, 1, tq, n_blk), lambda i, g, j: (i, g, j, 0)),
                  pl.BlockSpec((NSA_HPG, 1, tq, HEAD_DIM), lambda i, g, j: (g, i, j, 0)),
                  pl.BlockSpec((tq, LANES), lambda i, g, j: (i * nq + j, g)),
                  pl.BlockSpec((NSA_HPG, 1, HEAD_DIM), lambda i, g, j: (g, 0, 0))],
        out_specs=pl.BlockSpec((tq, NSA_HPG * HEAD_DIM), lambda i, g, j: (i * nq + j, g)),
        out_shape=jax.ShapeDtypeStruct((b * s, NSA_HEADS * HEAD_DIM), BF16),
        compiler_params=_cparams(("parallel", "parallel", "parallel")),
        name="nsa_selected_window",
    )(qn, kvn, kvn, kvn, kvn, selb, oc, misc, gain16)


def _fox_kernel(q_ref, k_ref, v_ref, nc_ref, gain_ref, out_ref, o_scr, *, tq):
    i = pl.program_id(1)
    q0 = i * tq
    tk = tq
    row = q0 + lax.broadcasted_iota(jnp.int32, (tq, 1), 0)

    def head(h, _):
        q = q_ref[h, 0]

        def tile(j, carry, causal):
            m, l, acc = carry
            start = pl.multiple_of(j * tk, tk)
            k = k_ref[h, 0, pl.ds(start, tk), :]
            v = v_ref[h, 0, pl.ds(start, tk), :]
            sc = _dot_nt(q, k) + nc_ref[0, h, j]
            if causal:
                kpos = start + lax.broadcasted_iota(jnp.int32, (1, tk), 1)
                sc = jnp.where(kpos <= row, sc, NEG_INF)
            m_new = jnp.maximum(m, jnp.max(sc, axis=-1, keepdims=True))
            alpha = jnp.exp(m - m_new)
            p = jnp.exp(sc - m_new)
            l = alpha * l + jnp.sum(p, axis=-1, keepdims=True)
            return m_new, l, alpha * acc + _dot(p.astype(BF16), v)

        init = (jnp.full((tq, 1), NEG_INF, F32), jnp.zeros((tq, 1), F32),
                jnp.zeros((tq, HEAD_DIM), F32))
        carry = lax.fori_loop(0, i, lambda j, c: tile(j, c, False), init)
        _, l, acc = tile(i, carry, True)
        o_scr[h] = _head_rms(acc / l, gain_ref[h])
        return 0

    lax.fori_loop(0, FOX_HEADS, head, 0)
    out_ref[...] = jnp.concatenate([o_scr[h] for h in range(FOX_HEADS)],
                                   axis=-1).astype(out_ref.dtype)


def _fox(fox, negc, gain16, b, s, tq):
    nq = s // tq
    kern = functools.partial(_fox_kernel, tq=tq)
    return pl.pallas_call(
        kern, grid=(b, nq),
        in_specs=[pl.BlockSpec((FOX_HEADS, 1, tq, HEAD_DIM), lambda i, j: (0, i, j, 0)),
                  pl.BlockSpec((FOX_HEADS, 1, s, HEAD_DIM), lambda i, j: (1, i, 0, 0)),
                  pl.BlockSpec((FOX_HEADS, 1, s, HEAD_DIM), lambda i, j: (2, i, 0, 0)),
                  pl.BlockSpec((1, FOX_HEADS, nq, 1, tq), lambda i, j: (i, 0, 0, 0, 0)),
                  pl.BlockSpec((FOX_HEADS, 1, HEAD_DIM), lambda i, j: (2, 0, 0))],
        out_specs=pl.BlockSpec((tq, FOX_HEADS * HEAD_DIM), lambda i, j: (i * nq + j, 0)),
        out_shape=jax.ShapeDtypeStruct((b * s, FOX_HEADS * HEAD_DIM), BF16),
        scratch_shapes=[pltpu.VMEM((FOX_HEADS, tq, HEAD_DIM), F32)],
        compiler_params=_cparams(("parallel", "parallel")),
        name="fox_attention",
    )(fox, fox, fox, negc, gain16)


def _sb_kernel(q_ref, k_ref, v_ref, gain_ref, out_ref, o_scr, *, tq, tk):
    i = pl.program_id(1)
    q0 = i * tq
    row = q0 + lax.broadcasted_iota(jnp.int32, (tq, 1), 0)
    r_i = lax.broadcasted_iota(jnp.int32, (tk, tk), 0)
    c_i = lax.broadcasted_iota(jnp.int32, (tk, tk), 1)
    suffix = jnp.where(r_i >= c_i, 1.0, 0.0).astype(BF16)
    n_diag = tq // tk
    n_full = q0 // tk

    def head(h, _):
        q = q_ref[h, 0]

        def tile(j, carry, masked):
            tail, acc = carry
            start = pl.multiple_of(j * tk, tk)
            k = k_ref[h, 0, pl.ds(start, tk), :]
            v = v_ref[h, 0, pl.ds(start, tk), :]
            z = _dot_nt(q, k)
            lr = -(jnp.maximum(z, 0.0) + jnp.log(1.0 + jnp.exp(-jnp.abs(z))))
            if masked:
                mask = (start + lax.broadcasted_iota(jnp.int32, (1, tk), 1)) < row
                lr = jnp.where(mask, lr, 0.0)
            hi = lr.astype(BF16)
            lo = (lr - hi.astype(F32)).astype(BF16)
            cs = _dot(hi, suffix) + _dot(lo, suffix)
            a = jnp.exp(z + cs + tail)
            if masked:
                a = jnp.where(mask, a, 0.0)
            return tail + cs[:, 0:1], acc + _dot(a.astype(BF16), v)

        carry = (jnp.zeros((tq, 1), F32), jnp.zeros((tq, HEAD_DIM), F32))
        for d in range(n_diag):
            carry = tile(n_full + n_diag - 1 - d, carry, True)
        _, acc = lax.fori_loop(0, n_full, lambda jj, c: tile(n_full - 1 - jj, c, False), carry)
        o_scr[h] = _head_rms(acc, gain_ref[h])
        return 0

    lax.fori_loop(0, SB_HEADS, head, 0)
    out_ref[...] = jnp.concatenate([o_scr[h] for h in range(SB_HEADS)],
                                   axis=-1).astype(out_ref.dtype)


def _stick_breaking(sb, gain16, b, s, tq, tk):
    nq = s // tq
    kern = functools.partial(_sb_kernel, tq=tq, tk=tk)
    return pl.pallas_call(
        kern, grid=(b, nq),
        in_specs=[pl.BlockSpec((SB_HEADS, 1, tq, HEAD_DIM), lambda i, j: (0, i, j, 0)),
                  pl.BlockSpec((SB_HEADS, 1, s, HEAD_DIM), lambda i, j: (1, i, 0, 0)),
                  pl.BlockSpec((SB_HEADS, 1, s, HEAD_DIM), lambda i, j: (2, i, 0, 0)),
                  pl.BlockSpec((SB_HEADS, 1, HEAD_DIM), lambda i, j: (3, 0, 0))],
        out_specs=pl.BlockSpec((tq, SB_HEADS * HEAD_DIM), lambda i, j: (i * nq + j, 0)),
        out_shape=jax.ShapeDtypeStruct((b * s, SB_HEADS * HEAD_DIM), BF16),
        scratch_shapes=[pltpu.VMEM((SB_HEADS, tq, HEAD_DIM), F32)],
        compiler_params=_cparams(("parallel", "parallel")),
        name="stick_breaking_attention",
    )(sb, sb, sb, gain16)


def _outproj_kernel(h_ref, on_ref, of_ref, os_ref, w_ref, out_ref):
    n1 = on_ref.shape[1]
    n2 = n1 + of_ref.shape[1]
    mix = (_dot(on_ref[...], w_ref[:n1, :]) + _dot(of_ref[...], w_ref[n1:n2, :])
           + _dot(os_ref[...], w_ref[n2:, :]))
    out_ref[...] = h_ref[...] + mix


def _out_projection(h2d, o_nsa, o_fox, o_sb, w, tm):
    t, d = h2d.shape
    row = lambda n: pl.BlockSpec((tm, n), lambda i: (i, 0))
    return pl.pallas_call(
        _outproj_kernel, grid=(t // tm,),
        in_specs=[row(d), row(o_nsa.shape[1]), row(o_fox.shape[1]), row(o_sb.shape[1]),
                  _const_spec(w.shape)],
        out_specs=row(d),
        out_shape=jax.ShapeDtypeStruct((t, d), F32),
        compiler_params=_cparams(("parallel",)),
        name="out_projection",
    )(h2d, o_nsa, o_fox, o_sb, w)


def _rms_rows(x, gain):
    return x * lax.rsqrt(jnp.mean(x * x, axis=-1, keepdims=True) + RMS_EPS) * gain


def _ffn_kernel(h_ref, g_ref, wg_ref, wu_ref, wd_ref, out_ref, *, tf):
    h = h_ref[...]
    hn = _rms_rows(h, g_ref[...]).astype(BF16)
    acc = h
    for c in range(wg_ref.shape[1] // tf):
        sl = slice(c * tf, (c + 1) * tf)
        gate = _dot(hn, wg_ref[:, sl])
        up = _dot(hn, wu_ref[:, sl])
        act = (gate * jax.nn.sigmoid(gate) * up).astype(BF16)
        acc = acc + _dot(act, wd_ref[sl, :])
    out_ref[...] = acc


def _ffn(h2d, gain, wg, wu, wd, tm, tf):
    t, d = h2d.shape
    row = pl.BlockSpec((tm, d), lambda i: (i, 0))
    return pl.pallas_call(
        functools.partial(_ffn_kernel, tf=tf), grid=(t // tm,),
        in_specs=[row, _const_spec((1, d)), _const_spec(wg.shape), _const_spec(wu.shape),
                  _const_spec(wd.shape)],
        out_specs=row,
        out_shape=jax.ShapeDtypeStruct((t, d), F32),
        compiler_params=_cparams(("parallel",)),
        name="swiglu_ffn",
    )(h2d, gain, wg, wu, wd)


def _ple_kernel(h_ref, p_ref, g_ref, wp_ref, wg_ref, gf_ref, out_ref, *, final):
    h = h_ref[...]
    hn = _rms_rows(h, g_ref[...]).astype(BF16)
    gate = jax.nn.sigmoid(_dot(hn, wg_ref[...]))
    new = h + _dot(p_ref[...].astype(BF16), wp_ref[...]) * gate
    if final:
        new = _rms_rows(new, gf_ref[...])
    out_ref[...] = new


def _ple(h2d, p2d, gain, wp, wg, gain_final, final, tm):
    t, d = h2d.shape
    row = lambda n: pl.BlockSpec((tm, n), lambda i: (i, 0))
    return pl.pallas_call(
        functools.partial(_ple_kernel, final=final), grid=(t // tm,),
        in_specs=[row(d), row(p2d.shape[1]), _const_spec((1, d)), _const_spec(wp.shape),
                  _const_spec(wg.shape), _const_spec((1, d))],
        out_specs=row(d),
        out_shape=jax.ShapeDtypeStruct((t, d), F32),
        compiler_params=_cparams(("parallel",)),
        name="per_layer_embedding",
    )(h2d, p2d, gain, wp, wg, gain_final)


def _overlap_matrix(n_pad, n_blk):
    n_cmp = n_pad - 1
    cmp_start = np.arange(n_pad) * CMP_STRIDE
    blk_start = np.arange(n_blk) * SLC_LEN
    ov = ((cmp_start[:, None] < blk_start[None, :] + SLC_LEN)
          & (cmp_start[:, None] + CMP_LEN > blk_start[None, :])
          & (np.arange(n_pad)[:, None] < n_cmp))
    return jnp.asarray(ov.astype(np.float32), dtype=BF16)


def _pick(n, pref):
    return pref if n % pref == 0 else n


def kernel(x, p, positions, norm_mix, w_in, b_nsa_gate, b_forget, nsa_cmp_pos_k, nsa_cmp_w1_k, nsa_cmp_w2_k, nsa_cmp_pos_v, nsa_cmp_w1_v, nsa_cmp_w2_v, head_norm, w_out, norm_ffn, w_ffn_gate, w_ffn_up, w_ffn_down, norm_ple, w_ple_proj, w_ple_gate, norm_final):
    b, s, d = x.shape
    depth = w_in.shape[0]
    t = b * s
    assert s % 512 == 0 and s >= WINDOW + 128
    tm = _pick(t, 512)
    tq_nsa, tk_nsa = 128, 256
    tq_fox = 512
    tq_sb, tk_sb = 256, 128
    n_chunk = s // CMP_STRIDE
    n_blk = s // SLC_LEN

    half = HEAD_DIM // 2
    inv_freq = ROPE_THETA ** (-jnp.arange(half, dtype=F32) / half)
    ang = positions.astype(F32)[..., None] * inv_freq
    cos, sin = jnp.cos(ang), jnp.sin(ang)
    cos_t = jnp.concatenate([cos] * 4, axis=-1).reshape(t, LANES)
    sin_t = jnp.concatenate([-sin, sin, -sin, sin], axis=-1).reshape(t, LANES)
    overlap = _overlap_matrix(n_chunk, n_blk)

    h = x.reshape(t, d)
    for i in range(depth):
        w_perm = _permute_w_in(w_in[i]).astype(BF16)
        bias = _misc_bias(b_nsa_gate[i], b_forget[i])
        qn, kvn, cmp4, fox, sb, misc = _in_projection(
            h, norm_mix[i].reshape(1, d), w_perm, cos_t, sin_t, bias, tm)

        width = CMP_STRIDE * HEAD_DIM
        pos = jnp.stack([nsa_cmp_pos_k[i], nsa_cmp_pos_v[i]]).reshape(2, 2, width)
        w1 = jnp.stack([nsa_cmp_w1_k[i], nsa_cmp_w1_v[i]]).astype(BF16)
        w2 = jnp.stack([nsa_cmp_w2_k[i], nsa_cmp_w2_v[i]]).astype(BF16)
        comp = _compress(cmp4, pos, w1, w2, b, s)

        gain16 = head_norm[i].reshape(N_HEADS, 1, HEAD_DIM)
        qn4 = qn.reshape(NSA_HEADS, b, s, HEAD_DIM)
        oc, selb = _nsa_cmp(qn4, comp, overlap, b, s, tq_nsa)
        o_nsa = _nsa_main(qn4, kvn.reshape(8, b, s, HEAD_DIM), selb, oc, misc, gain16,
                          b, s, tq_nsa, tk_nsa)

        log_f = misc[:, 12:16].reshape(b, s // LANES, LANES, FOX_HEADS).transpose(0, 3, 1, 2)
        negc = _neg_cumsum(log_f).reshape(b, FOX_HEADS, s // tq_fox, 1, tq_fox)
        o_fox = _fox(fox.reshape(12, b, s, HEAD_DIM), negc, gain16, b, s, tq_fox)
        o_sb = _stick_breaking(sb.reshape(12, b, s, HEAD_DIM), gain16, b, s, tq_sb, tk_sb)

        h = _out_projection(h, o_nsa, o_fox, o_sb, w_out[i].astype(BF16), tm)
        h = _ffn(h, norm_ffn[i].reshape(1, d), w_ffn_gate[i].astype(BF16),
                 w_ffn_up[i].astype(BF16), w_ffn_down[i].astype(BF16), tm, 256)
        h = _ple(h, p[i].reshape(t, -1), norm_ple[i].reshape(1, d), w_ple_proj[i].astype(BF16),
                 w_ple_gate[i].astype(BF16), norm_final.reshape(1, d), i == depth - 1, tm)
    return h.reshape(b, s, d)
```

```python
import functools

import numpy as np
import jax
import jax.numpy as jnp
from jax import lax
from jax.experimental import pallas as pl
from jax.experimental.pallas import tpu as pltpu

F32 = jnp.float32
BF16 = jnp.bfloat16

HEAD_DIM = 64
NSA_HEADS = 8
NSA_GROUPS = 2
NSA_HPG = 4
FOX_HEADS = 4
SB_HEADS = 4
N_HEADS = 16
CMP_LEN = 32
CMP_STRIDE = 16
CMP_HIDDEN = 128
SLC_LEN = 64
TOP_N = 16
WINDOW = 512
ROPE_THETA = 10000.0
RMS_EPS = 1e-6
NEG_INF = -1e30
FORCE = 1e9
TINY = 1e-20
Q_SCALE = HEAD_DIM ** -0.5

LANES = 128
IN_PAD = 3072
VMEM_LIMIT = 56 * 1024 * 1024

_OFF = dict(nq=0, nkc=512, nvc=640, nks=768, nvs=896, nkw=1024, nvw=1152, ngate=1280,
            fq=1304, fk=1560, fv=1816, ff=2072, sq=2076, sk=2332, sv=2588)


def _dot(a, b):
    return jnp.dot(a, b, preferred_element_type=F32)


def _dot_nt(a, b):
    return lax.dot_general(a, b, (((1,), (1,)), ((), ())), preferred_element_type=F32)


def _cparams(sem):
    return pltpu.CompilerParams(dimension_semantics=sem, vmem_limit_bytes=VMEM_LIMIT)


def _const_spec(shape):
    nd = len(shape)
    return pl.BlockSpec(shape, lambda *_: (0,) * nd, pipeline_mode=pl.Buffered(1))


def _permute_w_in(w):
    d = w.shape[0]

    def cols(name, width):
        return w[:, _OFF[name]:_OFF[name] + width]

    gate = cols("ngate", 24)
    zeros = lambda n: jnp.zeros((d, n), w.dtype)
    parts = [cols("nq", 512), cols("nkc", 128), cols("nks", 128), cols("nkw", 128),
             cols("nvc", 128), cols("nvs", 128), cols("nvw", 128),
             cols("fq", 256), cols("fk", 256), cols("fv", 256),
             cols("sq", 256), cols("sk", 256), cols("sv", 256),
             gate[:, :12], cols("ff", 4), zeros(112), gate[:, 12:], zeros(116)]
    return jnp.concatenate(parts, axis=1)


def _misc_bias(b_gate, b_forget):
    z = lambda n: jnp.zeros((n,), F32)
    return jnp.concatenate([b_gate[:12], b_forget, z(112), b_gate[12:], z(116)]).reshape(1, 256)


def _rope_half(x, cos, sin_signed):
    lane = lax.broadcasted_iota(jnp.int32, x.shape, 1)
    first = (lane % HEAD_DIM) < (HEAD_DIM // 2)
    partner = jnp.where(first, pltpu.roll(x, LANES - 32, 1), pltpu.roll(x, 32, 1))
    return x * cos + partner * sin_signed


def _log_sigmoid(x):
    return jnp.minimum(x, 0.0) - jnp.log(1.0 + jnp.exp(-jnp.abs(x)))


def _inproj_kernel(x_ref, g_ref, w_ref, cos_ref, sin_ref, bias_ref,
                   qn_ref, kvn_ref, cmp_ref, fox_ref, sb_ref, misc_ref):
    x = x_ref[...]
    ms = jnp.mean(x * x, axis=-1, keepdims=True)
    hn = (x * lax.rsqrt(ms + RMS_EPS) * g_ref[...]).astype(BF16)
    cos = cos_ref[...]
    sin = sin_ref[...]

    plan = [(qn_ref, 0, True, True), (qn_ref, 2, True, True),
            (qn_ref, 4, True, True), (qn_ref, 6, True, True),
            (cmp_ref, 0, True, False), (kvn_ref, 0, True, False),
            (kvn_ref, 2, True, False), (cmp_ref, 2, False, False),
            (kvn_ref, 4, False, False), (kvn_ref, 6, False, False),
            (fox_ref, 0, False, True), (fox_ref, 2, False, True),
            (fox_ref, 4, False, False), (fox_ref, 6, False, False),
            (fox_ref, 8, False, False), (fox_ref, 10, False, False),
            (sb_ref, 0, False, True), (sb_ref, 2, False, True),
            (sb_ref, 4, False, False), (sb_ref, 6, False, False),
            (sb_ref, 8, False, False), (sb_ref, 10, False, False)]
    for c in range(IN_PAD // 256):
        res = _dot(hn, w_ref[:, c * 256:(c + 1) * 256])
        if c == IN_PAD // 256 - 1:
            z = res + bias_ref[...]
            lane = lax.broadcasted_iota(jnp.int32, z.shape, 1)
            is_forget = (lane >= 12) & (lane < 16)
            misc_ref[...] = jnp.where(is_forget, _log_sigmoid(z), jax.nn.sigmoid(z))
            continue
        for half in range(2):
            dst, slot, rot, scl = plan[2 * c + half]
            y = res[:, half * LANES:(half + 1) * LANES]
            if rot:
                y = _rope_half(y, cos, sin)
            if scl:
                y = y * Q_SCALE
            y = y.astype(dst.dtype)
            dst[slot] = y[:, :HEAD_DIM]
            dst[slot + 1] = y[:, HEAD_DIM:]


def _in_projection(h2d, gain, w_perm, cos_t, sin_t, bias, tm):
    t, d = h2d.shape
    grid = (t // tm,)
    hm = lambda n, dt: jax.ShapeDtypeStruct((n, t, HEAD_DIM), dt)
    hm_spec = lambda n: pl.BlockSpec((n, tm, HEAD_DIM), lambda i: (0, i, 0))
    return pl.pallas_call(
        _inproj_kernel,
        grid=grid,
        in_specs=[pl.BlockSpec((tm, d), lambda i: (i, 0)),
                  _const_spec((1, d)),
                  _const_spec((d, IN_PAD)),
                  pl.BlockSpec((tm, LANES), lambda i: (i, 0)),
                  pl.BlockSpec((tm, LANES), lambda i: (i, 0)),
                  _const_spec((1, 256))],
        out_specs=[hm_spec(8), hm_spec(8), hm_spec(4), hm_spec(12), hm_spec(12),
                   pl.BlockSpec((tm, 256), lambda i: (i, 0))],
        out_shape=[hm(8, BF16), hm(8, BF16), hm(4, F32), hm(12, BF16), hm(12, BF16),
                   jax.ShapeDtypeStruct((t, 256), F32)],
        compiler_params=_cparams(("parallel",)),
        name="in_projection",
    )(h2d, gain, w_perm, cos_t, sin_t, bias)


def _gelu_tanh(x):
    c = np.float32(np.sqrt(2.0 / np.pi))
    return 0.5 * x * (1.0 + jnp.tanh(c * (x + 0.044715 * (x * x * x))))


def _compress_kernel(ch_ref, pos_ref, w1_ref, w2_ref, out_ref):
    ch = ch_ref[0, 0]
    half = CMP_STRIDE * HEAD_DIM
    top = _dot((ch + pos_ref[0, 0:1, :]).astype(BF16), w1_ref[0, :half, :])
    bot = _dot((ch + pos_ref[0, 1:2, :]).astype(BF16), w1_ref[0, half:, :])
    n = ch.shape[0]
    hidden = top + pltpu.roll(bot, n - 1, 0)
    act = _gelu_tanh(hidden).astype(BF16)
    out_ref[0, 0] = _dot(act, w2_ref[0]).astype(out_ref.dtype)


def _compress(cmp4, pos, w1, w2, b, s):
    n_chunk = s // CMP_STRIDE
    width = CMP_STRIDE * HEAD_DIM
    chunks = cmp4.reshape(4, b, n_chunk, width)
    return pl.pallas_call(
        _compress_kernel,
        grid=(4, b),
        in_specs=[pl.BlockSpec((1, 1, n_chunk, width), lambda w, i: (w, i, 0, 0)),
                  pl.BlockSpec((1, 2, width), lambda w, i: (w // 2, 0, 0)),
                  pl.BlockSpec((1, 2 * width, CMP_HIDDEN), lambda w, i: (w // 2, 0, 0)),
                  pl.BlockSpec((1, CMP_HIDDEN, HEAD_DIM), lambda w, i: (w // 2, 0, 0))],
        out_specs=pl.BlockSpec((1, 1, n_chunk, HEAD_DIM), lambda w, i: (w, i, 0, 0)),
        out_shape=jax.ShapeDtypeStruct((4, b, n_chunk, HEAD_DIM), BF16),
        compiler_params=_cparams(("parallel", "parallel")),
        name="nsa_compress",
    )(chunks, pos, w1, w2)


def _split3(x):
    h1 = x.astype(BF16)
    r1 = x - h1.astype(F32)
    h2 = r1.astype(BF16)
    h3 = (r1 - h2.astype(F32)).astype(BF16)
    return h1, h2, h3


def _cumsum_kernel(lf_ref, out_ref):
    nh, r, _ = lf_ref.shape[1:]
    row = lax.broadcasted_iota(jnp.int32, (LANES, LANES), 0)
    col = lax.broadcasted_iota(jnp.int32, (LANES, LANES), 1)
    upper = jnp.where(row <= col, 1.0, 0.0).astype(BF16)
    rr = lax.broadcasted_iota(jnp.int32, (r, r), 0)
    rc = lax.broadcasted_iota(jnp.int32, (r, r), 1)
    strict = jnp.where(rc < rr, 1.0, 0.0).astype(BF16)
    for h in range(nh):
        x = lf_ref[0, h]
        within = sum(_dot(part, upper) for part in _split3(x))
        tot = jnp.broadcast_to(within[:, LANES - 1:LANES], (r, LANES))
        before = sum(_dot(strict, part) for part in _split3(tot))
        out_ref[0, h] = -(within + before)


def _neg_cumsum(lf):
    b, nh, r, _ = lf.shape
    spec = pl.BlockSpec((1, nh, r, LANES), lambda i: (i, 0, 0, 0))
    return pl.pallas_call(
        _cumsum_kernel, grid=(b,), in_specs=[spec], out_specs=spec,
        out_shape=jax.ShapeDtypeStruct(lf.shape, F32),
        compiler_params=_cparams(("parallel",)),
        name="forget_cumsum",
    )(lf)


def _nsa_cmp_kernel(q_ref, kc_ref, vc_ref, ovt_ref, oc_ref, selb_ref, *, tq, n_sel):
    q0 = pl.program_id(2) * tq
    q4 = q_ref[:, 0].reshape(NSA_HPG * tq, HEAD_DIM)
    kc = kc_ref[0, 0]
    vc = vc_ref[0, 0]
    n_pad = kc.shape[0]
    n_blk = ovt_ref.shape[0]
    s = _dot_nt(q4, kc).reshape(NSA_HPG, tq, n_pad)
    t = q0 + lax.broadcasted_iota(jnp.int32, (1, tq, 1), 1)
    n = lax.broadcasted_iota(jnp.int32, (1, 1, n_pad), 2)
    mask = (n * CMP_STRIDE + (CMP_LEN - 1) <= t) & (n < n_pad - 1)
    l = jnp.where(mask, s, NEG_INF)
    m = jnp.max(l, axis=-1, keepdims=True)
    e = jnp.where(mask, jnp.exp(l - m), 0.0)
    p = e / jnp.maximum(jnp.sum(e, axis=-1, keepdims=True), TINY)
    pb = p.astype(BF16).reshape(NSA_HPG * tq, n_pad)
    oc_ref[:, 0] = _dot(pb, vc).reshape(NSA_HPG, tq, HEAD_DIM)
    imp4 = _dot_nt(ovt_ref[...], pb)
    imp = sum(imp4[:, hh * tq:(hh + 1) * tq] for hh in range(NSA_HPG))
    t2 = q0 + lax.broadcasted_iota(jnp.int32, (1, tq), 1)
    blk = lax.broadcasted_iota(jnp.int32, (n_blk, tq), 0)
    cur = t2 // SLC_LEN
    forced = (blk == 0) | (blk == cur) | (blk == cur - 1)
    score = jnp.where(forced, FORCE, jnp.where(blk <= cur, imp, -FORCE))

    sub = lax.broadcasted_iota(jnp.int32, (8, tq), 0)
    groups = [score[8 * v:8 * v + 8] for v in range(n_blk // 8)]
    rank = [jnp.zeros((8, tq), F32) for _ in groups]
    for i in range(n_blk):
        row = jnp.broadcast_to(score[i:i + 1], (8, tq))
        for v, grp in enumerate(groups):
            if v > i // 8:
                beats = jnp.where(row >= grp, 1.0, 0.0)
            elif v < i // 8:
                beats = jnp.where(row > grp, 1.0, 0.0)
            else:
                beats = jnp.where(sub > i % 8, jnp.where(row >= grp, 1.0, 0.0),
                                  jnp.where(row > grp, 1.0, 0.0))
            rank[v] = rank[v] + beats
    sel_bias = jnp.where(jnp.concatenate(rank, axis=0) < float(n_sel), 0.0, NEG_INF)
    selb_ref[0, 0] = sel_bias.T.astype(selb_ref.dtype)


def _nsa_cmp(qn, comp, overlap, b, s, tq):
    n_pad = comp.shape[2]
    n_blk = overlap.shape[0]
    grid = (b, NSA_GROUPS, s // tq)
    kern = functools.partial(_nsa_cmp_kernel, tq=tq, n_sel=min(TOP_N, n_blk))
    return pl.pallas_call(
        kern, grid=grid,
        in_specs=[pl.BlockSpec((NSA_HPG, 1, tq, HEAD_DIM), lambda i, g, j: (g, i, j, 0)),
                  pl.BlockSpec((1, 1, n_pad, HEAD_DIM), lambda i, g, j: (g, i, 0, 0)),
                  pl.BlockSpec((1, 1, n_pad, HEAD_DIM), lambda i, g, j: (2 + g, i, 0, 0)),
                  _const_spec((n_blk, n_pad))],
        out_specs=[pl.BlockSpec((NSA_HPG, 1, tq, HEAD_DIM), lambda i, g, j: (g, i, j, 0)),
                   pl.BlockSpec((1, 1, tq, n_blk), lambda i, g, j: (i, g, j, 0))],
        out_shape=[jax.ShapeDtypeStruct((NSA_HEADS, b, s, HEAD_DIM), F32),
                   jax.ShapeDtypeStruct((b, NSA_GROUPS, s, n_blk), BF16)],
        compiler_params=_cparams(("parallel", "parallel", "parallel")),
        name="nsa_compressed_select",
    )(qn, comp, comp, overlap)


def _head_rms(o, gain):
    return o * lax.rsqrt(jnp.mean(o * o, axis=-1, keepdims=True) + RMS_EPS) * gain


def _nsa_main_kernel(q_ref, ks_ref, kw_ref, vs_ref, vw_ref, selb_ref, oc_ref, gate_ref, gain_ref,
                     out_ref, *, tq, tk):
    q0 = pl.program_id(2) * tq
    hq = NSA_HPG * tq
    q4 = q_ref[:, 0].reshape(hq, HEAD_DIM)
    selb = selb_ref[0, 0]
    t = q0 + lax.broadcasted_iota(jnp.int32, (tq, 1), 0)
    q_sel = jnp.concatenate(
        [jnp.concatenate([q_ref[hh, 0], selb], axis=-1) for hh in range(NSA_HPG)], axis=0)

    def slc_tile(j, carry, causal):
        m, l, acc = carry
        start = pl.multiple_of(j * tk, tk)
        k = ks_ref[0, 0, pl.ds(start, tk), :]
        v = vs_ref[0, 0, pl.ds(start, tk), :]
        sc = _dot_nt(q_sel, k).reshape(NSA_HPG, tq, tk)
        if causal:
            kpos = start + lax.broadcasted_iota(jnp.int32, (1, tk), 1)
            sc = sc + jnp.where(kpos <= t, 0.0, NEG_INF)[None]
        m_new = jnp.maximum(m, jnp.max(sc, axis=-1, keepdims=True))
        alpha = jnp.exp(m - m_new)
        p = jnp.exp(sc - m_new)
        l = alpha * l + jnp.sum(p, axis=-1, keepdims=True)
        pv = _dot(p.astype(BF16).reshape(hq, tk), v).reshape(NSA_HPG, tq, HEAD_DIM)
        return m_new, l, alpha * acc + pv

    init = (jnp.full((NSA_HPG, tq, 1), NEG_INF, F32), jnp.zeros((NSA_HPG, tq, 1), F32),
            jnp.zeros((NSA_HPG, tq, HEAD_DIM), F32))
    n_full = q0 // tk
    carry = lax.fori_loop(0, n_full, lambda j, c: slc_tile(j, c, False), init)
    _, l_s, acc_s = slc_tile(n_full, carry, True)
    o_s = acc_s / l_s

    span = WINDOW + tq
    w0 = pl.multiple_of(jnp.maximum(q0 - WINDOW, 0), tq)
    kwin = kw_ref[0, 0, pl.ds(w0, span), :]
    vwin = vw_ref[0, 0, pl.ds(w0, span), :]
    kpos = w0 + lax.broadcasted_iota(jnp.int32, (1, span), 1)
    wbias = jnp.where((kpos <= t) & (t - kpos < WINDOW), 0.0, NEG_INF)
    sw = _dot_nt(q4, kwin).reshape(NSA_HPG, tq, span) + wbias[None]
    mw = jnp.max(sw, axis=-1, keepdims=True)
    ew = jnp.exp(sw - mw)
    lw = jnp.sum(ew, axis=-1, keepdims=True)
    o_w = _dot(ew.astype(BF16).reshape(hq, span), vwin).reshape(NSA_HPG, tq, HEAD_DIM) / lw

    gates = gate_ref[...]
    heads = []
    for hh in range(NSA_HPG):
        g0 = gates[:, 3 * hh:3 * hh + 1]
        g1 = gates[:, 3 * hh + 1:3 * hh + 2]
        g2 = gates[:, 3 * hh + 2:3 * hh + 3]
        o = g0 * oc_ref[hh, 0] + g1 * o_s[hh] + g2 * o_w[hh]
        heads.append(_head_rms(o, gain_ref[hh]).astype(out_ref.dtype))
    out_ref[...] = jnp.concatenate(heads, axis=-1)


def _nsa_main(qn, ks_sel, kvn, selb, oc, misc, gain16, b, s, tq, tk):
    nq = s // tq
    n_blk = selb.shape[-1]
    grid = (b, NSA_GROUPS, nq)
    kv_spec = lambda base: pl.BlockSpec((1, 1, s, HEAD_DIM), lambda i, g, j: (base + g, i, 0, 0))
    kern = functools.partial(_nsa_main_kernel, tq=tq, tk=tk)
    return pl.pallas_call(
        kern, grid=grid,
        in_specs=[pl.BlockSpec((NSA_HPG, 1, tq, HEAD_DIM), lambda i, g, j: (g, i, j, 0)),
                  pl.BlockSpec((1, 1, s, HEAD_DIM + n_blk), lambda i, g, j: (g, i, 0, 0)),
                  kv_spec(2), kv_spec(4), kv_spec(6),
                  pl.BlockSpec((1, 1, tq, n_blk), lambda i, g, j: (i, g, j, 0)),
                  pl.BlockSpec((NSA_HPG, 1, tq, HEAD_DIM), lambda i, g, j: (g, i, j, 0)),
                  pl.BlockSpec((tq, LANES), lambda i, g, j: (i * nq + j, g)),
                  pl.BlockSpec((NSA_HPG, 1, HEAD_DIM), lambda i, g, j: (g, 0, 0))],
        out_specs=pl.BlockSpec((tq, NSA_HPG * HEAD_DIM), lambda i, g, j: (i * nq + j, g)),
        out_shape=jax.ShapeDtypeStruct((b * s, NSA_HEADS * HEAD_DIM), BF16),
        compiler_params=_cparams(("parallel", "parallel", "parallel")),
        name="nsa_selected_window",
    )(qn, ks_sel, kvn, kvn, kvn, selb, oc, misc, gain16)


def _fox_kernel(q_ref, k_ref, v_ref, nc_ref, gain_ref, out_ref, o_scr, *, tq):
    i = pl.program_id(1)
    q0 = i * tq
    tk = tq
    row = q0 + lax.broadcasted_iota(jnp.int32, (tq, 1), 0)

    def head(h, _):
        q = q_ref[h, 0]

        def tile(j, carry, causal):
            m, l, acc = carry
            start = pl.multiple_of(j * tk, tk)
            k = k_ref[h, 0, pl.ds(start, tk), :]
            v = v_ref[h, 0, pl.ds(start, tk), :]
            sc = _dot_nt(q, k) + nc_ref[0, h, j]
            if causal:
                kpos = start + lax.broadcasted_iota(jnp.int32, (1, tk), 1)
                sc = jnp.where(kpos <= row, sc, NEG_INF)
            m_new = jnp.maximum(m, jnp.max(sc, axis=-1, keepdims=True))
            alpha = jnp.exp(m - m_new)
            p = jnp.exp(sc - m_new)
            l = alpha * l + jnp.sum(p, axis=-1, keepdims=True)
            return m_new, l, alpha * acc + _dot(p.astype(BF16), v)

        init = (jnp.full((tq, 1), NEG_INF, F32), jnp.zeros((tq, 1), F32),
                jnp.zeros((tq, HEAD_DIM), F32))
        carry = lax.fori_loop(0, i, lambda j, c: tile(j, c, False), init)
        _, l, acc = tile(i, carry, True)
        o_scr[h] = _head_rms(acc / l, gain_ref[h])
        return 0

    lax.fori_loop(0, FOX_HEADS, head, 0)
    out_ref[...] = jnp.concatenate([o_scr[h] for h in range(FOX_HEADS)],
                                   axis=-1).astype(out_ref.dtype)


def _fox(fox, negc, gain16, b, s, tq):
    nq = s // tq
    kern = functools.partial(_fox_kernel, tq=tq)
    return pl.pallas_call(
        kern, grid=(b, nq),
        in_specs=[pl.BlockSpec((FOX_HEADS, 1, tq, HEAD_DIM), lambda i, j: (0, i, j, 0)),
                  pl.BlockSpec((FOX_HEADS, 1, s, HEAD_DIM), lambda i, j: (1, i, 0, 0)),
                  pl.BlockSpec((FOX_HEADS, 1, s, HEAD_DIM), lambda i, j: (2, i, 0, 0)),
                  pl.BlockSpec((1, FOX_HEADS, nq, 1, tq), lambda i, j: (i, 0, 0, 0, 0)),
                  pl.BlockSpec((FOX_HEADS, 1, HEAD_DIM), lambda i, j: (2, 0, 0))],
        out_specs=pl.BlockSpec((tq, FOX_HEADS * HEAD_DIM), lambda i, j: (i * nq + j, 0)),
        out_shape=jax.ShapeDtypeStruct((b * s, FOX_HEADS * HEAD_DIM), BF16),
        scratch_shapes=[pltpu.VMEM((FOX_HEADS, tq, HEAD_DIM), F32)],
        compiler_params=_cparams(("parallel", "parallel")),
        name="fox_attention",
    )(fox, fox, fox, negc, gain16)


def _sb_kernel(q_ref, k_ref, v_ref, gain_ref, out_ref, acc_ref, *, tq, tk):
    i = pl.program_id(1)
    q0 = i * tq
    row = q0 + lax.broadcasted_iota(jnp.int32, (tq, 1), 0)
    r_i = lax.broadcasted_iota(jnp.int32, (tk, tk), 0)
    c_i = lax.broadcasted_iota(jnp.int32, (tk, tk), 1)
    suffix = jnp.where(r_i >= c_i, 1.0, 0.0).astype(BF16)
    n_diag = tq // tk
    n_full = q0 // tk
    heads = range(SB_HEADS)
    qs = [q_ref[h, 0] for h in heads]
    acc_ref[...] = jnp.zeros_like(acc_ref)

    def tile(h, j, tail, masked):
        start = pl.multiple_of(j * tk, tk)
        k = k_ref[h, 0, pl.ds(start, tk), :]
        v = v_ref[h, 0, pl.ds(start, tk), :]
        z = _dot_nt(qs[h], k)
        lr = -(jnp.maximum(z, 0.0) + jnp.log(1.0 + jnp.exp(-jnp.abs(z))))
        if masked:
            mask = (start + lax.broadcasted_iota(jnp.int32, (1, tk), 1)) < row
            lr = jnp.where(mask, lr, 0.0)
        hi = lr.astype(BF16)
        lo = (lr - hi.astype(F32)).astype(BF16)
        cs = _dot(hi, suffix) + _dot(lo, suffix)
        a = jnp.exp(z + cs + tail)
        if masked:
            a = jnp.where(mask, a, 0.0)
        acc_ref[h] += _dot(a.astype(BF16), v)
        return tail + cs[:, 0:1]

    tails = tuple(jnp.zeros((tq, 1), F32) for _ in heads)
    for d in range(n_diag):
        tails = tuple(tile(h, n_full + n_diag - 1 - d, tails[h], True) for h in heads)
    lax.fori_loop(0, n_full,
                  lambda jj, ts: tuple(tile(h, n_full - 1 - jj, ts[h], False) for h in heads),
                  tails)
    out_ref[...] = jnp.concatenate([_head_rms(acc_ref[h], gain_ref[h]) for h in heads],
                                   axis=-1).astype(out_ref.dtype)


def _stick_breaking(sb, gain16, b, s, tq, tk):
    nq = s // tq
    kern = functools.partial(_sb_kernel, tq=tq, tk=tk)
    return pl.pallas_call(
        kern, grid=(b, nq),
        in_specs=[pl.BlockSpec((SB_HEADS, 1, tq, HEAD_DIM), lambda i, j: (0, i, j, 0)),
                  pl.BlockSpec((SB_HEADS, 1, s, HEAD_DIM), lambda i, j: (1, i, 0, 0)),
                  pl.BlockSpec((SB_HEADS, 1, s, HEAD_DIM), lambda i, j: (2, i, 0, 0)),
                  pl.BlockSpec((SB_HEADS, 1, HEAD_DIM), lambda i, j: (3, 0, 0))],
        out_specs=pl.BlockSpec((tq, SB_HEADS * HEAD_DIM), lambda i, j: (i * nq + j, 0)),
        out_shape=jax.ShapeDtypeStruct((b * s, SB_HEADS * HEAD_DIM), BF16),
        scratch_shapes=[pltpu.VMEM((SB_HEADS, tq, HEAD_DIM), F32)],
        compiler_params=_cparams(("parallel", "parallel")),
        name="stick_breaking_attention",
    )(sb, sb, sb, gain16)


def _outproj_kernel(h_ref, on_ref, of_ref, os_ref, w_ref, out_ref):
    n1 = on_ref.shape[1]
    n2 = n1 + of_ref.shape[1]
    mix = (_dot(on_ref[...], w_ref[:n1, :]) + _dot(of_ref[...], w_ref[n1:n2, :])
           + _dot(os_ref[...], w_ref[n2:, :]))
    out_ref[...] = h_ref[...] + mix


def _out_projection(h2d, o_nsa, o_fox, o_sb, w, tm):
    t, d = h2d.shape
    row = lambda n: pl.BlockSpec((tm, n), lambda i: (i, 0))
    return pl.pallas_call(
        _outproj_kernel, grid=(t // tm,),
        in_specs=[row(d), row(o_nsa.shape[1]), row(o_fox.shape[1]), row(o_sb.shape[1]),
                  _const_spec(w.shape)],
        out_specs=row(d),
        out_shape=jax.ShapeDtypeStruct((t, d), F32),
        compiler_params=_cparams(("parallel",)),
        name="out_projection",
    )(h2d, o_nsa, o_fox, o_sb, w)


def _rms_rows(x, gain):
    return x * lax.rsqrt(jnp.mean(x * x, axis=-1, keepdims=True) + RMS_EPS) * gain


def _ffn_kernel(h_ref, g_ref, wg_ref, wu_ref, wd_ref, out_ref, *, tf):
    h = h_ref[...]
    hn = _rms_rows(h, g_ref[...]).astype(BF16)
    acc = h
    for c in range(wg_ref.shape[1] // tf):
        sl = slice(c * tf, (c + 1) * tf)
        gate = _dot(hn, wg_ref[:, sl])
        up = _dot(hn, wu_ref[:, sl])
        act = (gate * jax.nn.sigmoid(gate) * up).astype(BF16)
        acc = acc + _dot(act, wd_ref[sl, :])
    out_ref[...] = acc


def _ffn(h2d, gain, wg, wu, wd, tm, tf):
    t, d = h2d.shape
    row = pl.BlockSpec((tm, d), lambda i: (i, 0))
    return pl.pallas_call(
        functools.partial(_ffn_kernel, tf=tf), grid=(t // tm,),
        in_specs=[row, _const_spec((1, d)), _const_spec(wg.shape), _const_spec(wu.shape),
                  _const_spec(wd.shape)],
        out_specs=row,
        out_shape=jax.ShapeDtypeStruct((t, d), F32),
        compiler_params=_cparams(("parallel",)),
        name="swiglu_ffn",
    )(h2d, gain, wg, wu, wd)


def _ple_kernel(h_ref, p_ref, g_ref, wp_ref, wg_ref, gf_ref, out_ref, *, final):
    h = h_ref[...]
    hn = _rms_rows(h, g_ref[...]).astype(BF16)
    gate = jax.nn.sigmoid(_dot(hn, wg_ref[...]))
    new = h + _dot(p_ref[...].astype(BF16), wp_ref[...]) * gate
    if final:
        new = _rms_rows(new, gf_ref[...])
    out_ref[...] = new


def _ple(h2d, p2d, gain, wp, wg, gain_final, final, tm):
    t, d = h2d.shape
    row = lambda n: pl.BlockSpec((tm, n), lambda i: (i, 0))
    return pl.pallas_call(
        functools.partial(_ple_kernel, final=final), grid=(t // tm,),
        in_specs=[row(d), row(p2d.shape[1]), _const_spec((1, d)), _const_spec(wp.shape),
                  _const_spec(wg.shape), _const_spec((1, d))],
        out_specs=row(d),
        out_shape=jax.ShapeDtypeStruct((t, d), F32),
        compiler_params=_cparams(("parallel",)),
        name="per_layer_embedding",
    )(h2d, p2d, gain, wp, wg, gain_final)


def _overlap_matrix(n_pad, n_blk):
    n_cmp = n_pad - 1
    cmp_start = np.arange(n_pad) * CMP_STRIDE
    blk_start = np.arange(n_blk) * SLC_LEN
    ov = ((cmp_start[:, None] < blk_start[None, :] + SLC_LEN)
          & (cmp_start[:, None] + CMP_LEN > blk_start[None, :])
          & (np.arange(n_pad)[:, None] < n_cmp))
    return jnp.asarray(ov.T.astype(np.float32), dtype=BF16)


def _pick(n, pref):
    return pref if n % pref == 0 else n


def kernel(x, p, positions, norm_mix, w_in, b_nsa_gate, b_forget, nsa_cmp_pos_k, nsa_cmp_w1_k, nsa_cmp_w2_k, nsa_cmp_pos_v, nsa_cmp_w1_v, nsa_cmp_w2_v, head_norm, w_out, norm_ffn, w_ffn_gate, w_ffn_up, w_ffn_down, norm_ple, w_ple_proj, w_ple_gate, norm_final):
    b, s, d = x.shape
    depth = w_in.shape[0]
    t = b * s
    assert s % 512 == 0 and s >= WINDOW + 128
    tm = _pick(t, 512)
    tq_nsa, tk_nsa = 128, 256
    tq_cmp = 256
    tq_fox = 512
    tq_sb, tk_sb = 256, 128
    n_chunk = s // CMP_STRIDE
    n_blk = s // SLC_LEN

    half = HEAD_DIM // 2
    inv_freq = ROPE_THETA ** (-jnp.arange(half, dtype=F32) / half)
    ang = positions.astype(F32)[..., None] * inv_freq
    cos, sin = jnp.cos(ang), jnp.sin(ang)
    cos_t = jnp.concatenate([cos] * 4, axis=-1).reshape(t, LANES)
    sin_t = jnp.concatenate([-sin, sin, -sin, sin], axis=-1).reshape(t, LANES)
    overlap = _overlap_matrix(n_chunk, n_blk)
    blk_onehot = jnp.asarray(
        (np.arange(s)[:, None] // SLC_LEN == np.arange(n_blk)[None, :]).astype(np.float32), dtype=BF16)

    h = x.reshape(t, d)
    for i in range(depth):
        w_perm = _permute_w_in(w_in[i]).astype(BF16)
        bias = _misc_bias(b_nsa_gate[i], b_forget[i])
        qn, kvn, cmp4, fox, sb, misc = _in_projection(
            h, norm_mix[i].reshape(1, d), w_perm, cos_t, sin_t, bias, tm)

        width = CMP_STRIDE * HEAD_DIM
        pos = jnp.stack([nsa_cmp_pos_k[i], nsa_cmp_pos_v[i]]).reshape(2, 2, width)
        w1 = jnp.stack([nsa_cmp_w1_k[i], nsa_cmp_w1_v[i]]).astype(BF16)
        w2 = jnp.stack([nsa_cmp_w2_k[i], nsa_cmp_w2_v[i]]).astype(BF16)
        comp = _compress(cmp4, pos, w1, w2, b, s)

        gain16 = head_norm[i].reshape(N_HEADS, 1, HEAD_DIM)
        qn4 = qn.reshape(NSA_HEADS, b, s, HEAD_DIM)
        oc, selb = _nsa_cmp(qn4, comp, overlap, b, s, tq_cmp)
        kvn4 = kvn.reshape(8, b, s, HEAD_DIM)
        ks_sel = jnp.concatenate(
            [kvn4[:NSA_GROUPS], jnp.broadcast_to(blk_onehot, (NSA_GROUPS, b, s, n_blk))], axis=-1)
        o_nsa = _nsa_main(qn4, ks_sel, kvn4, selb, oc, misc, gain16, b, s, tq_nsa, tk_nsa)

        log_f = misc[:, 12:16].reshape(b, s // LANES, LANES, FOX_HEADS).transpose(0, 3, 1, 2)
        negc = _neg_cumsum(log_f).reshape(b, FOX_HEADS, s // tq_fox, 1, tq_fox)
        o_fox = _fox(fox.reshape(12, b, s, HEAD_DIM), negc, gain16, b, s, tq_fox)
        o_sb = _stick_breaking(sb.reshape(12, b, s, HEAD_DIM), gain16, b, s, tq_sb, tk_sb)

        h = _out_projection(h, o_nsa, o_fox, o_sb, w_out[i].astype(BF16), tm)
        h = _ffn(h, norm_ffn[i].reshape(1, d), w_ffn_gate[i].astype(BF16),
                 w_ffn_up[i].astype(BF16), w_ffn_down[i].astype(BF16), tm, 256)
        h = _ple(h, p[i].reshape(t, -1), norm_ple[i].reshape(1, d), w_ple_proj[i].astype(BF16),
                 w_ple_gate[i].astype(BF16), norm_final.reshape(1, d), i == depth - 1, tm)
    return h.reshape(b, s, d)
```

```python
import functools

import numpy as np
import jax
import jax.numpy as jnp
from jax import lax
from jax.experimental import pallas as pl
from jax.experimental.pallas import tpu as pltpu

F32 = jnp.float32
BF16 = jnp.bfloat16

HEAD_DIM = 64
NSA_HEADS = 8
NSA_GROUPS = 2
NSA_HPG = 4
FOX_HEADS = 4
SB_HEADS = 4
N_HEADS = 16
CMP_LEN = 32
CMP_STRIDE = 16
CMP_HIDDEN = 128
SLC_LEN = 64
TOP_N = 16
WINDOW = 512
ROPE_THETA = 10000.0
RMS_EPS = 1e-6
NEG_INF = -1e30
FORCE = 1e9
TINY = 1e-20
Q_SCALE = HEAD_DIM ** -0.5

LANES = 128
IN_PAD = 3072
VMEM_LIMIT = 56 * 1024 * 1024

_OFF = dict(nq=0, nkc=512, nvc=640, nks=768, nvs=896, nkw=1024, nvw=1152, ngate=1280,
            fq=1304, fk=1560, fv=1816, ff=2072, sq=2076, sk=2332, sv=2588)


def _dot(a, b):
    return jnp.dot(a, b, preferred_element_type=F32)


def _dot_nt(a, b):
    return lax.dot_general(a, b, (((1,), (1,)), ((), ())), preferred_element_type=F32)


def _cparams(sem):
    return pltpu.CompilerParams(dimension_semantics=sem, vmem_limit_bytes=VMEM_LIMIT)


def _const_spec(shape):
    nd = len(shape)
    return pl.BlockSpec(shape, lambda *_: (0,) * nd, pipeline_mode=pl.Buffered(1))


def _permute_w_in(w):
    d = w.shape[0]

    def cols(name, width):
        return w[:, _OFF[name]:_OFF[name] + width]

    gate = cols("ngate", 24)
    zeros = lambda n: jnp.zeros((d, n), w.dtype)
    parts = [cols("nq", 512), cols("nkc", 128), cols("nks", 128), cols("nkw", 128),
             cols("nvc", 128), cols("nvs", 128), cols("nvw", 128),
             cols("fq", 256), cols("fk", 256), cols("fv", 256),
             cols("sq", 256), cols("sk", 256), cols("sv", 256),
             gate[:, :12], cols("ff", 4), zeros(112), gate[:, 12:], zeros(116)]
    return jnp.concatenate(parts, axis=1)


def _misc_bias(b_gate, b_forget):
    z = lambda n: jnp.zeros((n,), F32)
    return jnp.concatenate([b_gate[:12], b_forget, z(112), b_gate[12:], z(116)]).reshape(1, 256)


def _rope_half(x, cos, sin_signed):
    lane = lax.broadcasted_iota(jnp.int32, x.shape, 1)
    first = (lane % HEAD_DIM) < (HEAD_DIM // 2)
    partner = jnp.where(first, pltpu.roll(x, LANES - 32, 1), pltpu.roll(x, 32, 1))
    return x * cos + partner * sin_signed


def _log_sigmoid(x):
    return jnp.minimum(x, 0.0) - jnp.log(1.0 + jnp.exp(-jnp.abs(x)))


def _inproj_kernel(x_ref, g_ref, w_ref, cos_ref, sin_ref, bias_ref,
                   qn_ref, kvn_ref, cmp_ref, fox_ref, sb_ref, misc_ref):
    x = x_ref[...]
    ms = jnp.mean(x * x, axis=-1, keepdims=True)
    hn = (x * lax.rsqrt(ms + RMS_EPS) * g_ref[...]).astype(BF16)
    cos = cos_ref[...]
    sin = sin_ref[...]

    plan = [(qn_ref, 0, True, True), (qn_ref, 2, True, True),
            (qn_ref, 4, True, True), (qn_ref, 6, True, True),
            (cmp_ref, 0, True, False), (kvn_ref, 0, True, False),
            (kvn_ref, 2, True, False), (cmp_ref, 2, False, False),
            (kvn_ref, 4, False, False), (kvn_ref, 6, False, False),
            (fox_ref, 0, False, True), (fox_ref, 2, False, True),
            (fox_ref, 4, False, False), (fox_ref, 6, False, False),
            (fox_ref, 8, False, False), (fox_ref, 10, False, False),
            (sb_ref, 0, False, True), (sb_ref, 2, False, True),
            (sb_ref, 4, False, False), (sb_ref, 6, False, False),
            (sb_ref, 8, False, False), (sb_ref, 10, False, False)]
    for c in range(IN_PAD // 256):
        res = _dot(hn, w_ref[:, c * 256:(c + 1) * 256])
        if c == IN_PAD // 256 - 1:
            z = res + bias_ref[...]
            lane = lax.broadcasted_iota(jnp.int32, z.shape, 1)
            is_forget = (lane >= 12) & (lane < 16)
            misc_ref[...] = jnp.where(is_forget, _log_sigmoid(z), jax.nn.sigmoid(z))
            continue
        for half in range(2):
            dst, slot, rot, scl = plan[2 * c + half]
            y = res[:, half * LANES:(half + 1) * LANES]
            if rot:
                y = _rope_half(y, cos, sin)
            if scl:
                y = y * Q_SCALE
            y = y.astype(dst.dtype)
            dst[slot] = y[:, :HEAD_DIM]
            dst[slot + 1] = y[:, HEAD_DIM:]


def _in_projection(h2d, gain, w_perm, cos_t, sin_t, bias, tm):
    t, d = h2d.shape
    grid = (t // tm,)
    hm = lambda n, dt: jax.ShapeDtypeStruct((n, t, HEAD_DIM), dt)
    hm_spec = lambda n: pl.BlockSpec((n, tm, HEAD_DIM), lambda i: (0, i, 0))
    return pl.pallas_call(
        _inproj_kernel,
        grid=grid,
        in_specs=[pl.BlockSpec((tm, d), lambda i: (i, 0)),
                  _const_spec((1, d)),
                  _const_spec((d, IN_PAD)),
                  pl.BlockSpec((tm, LANES), lambda i: (i, 0)),
                  pl.BlockSpec((tm, LANES), lambda i: (i, 0)),
                  _const_spec((1, 256))],
        out_specs=[hm_spec(8), hm_spec(8), hm_spec(4), hm_spec(12), hm_spec(12),
                   pl.BlockSpec((tm, 256), lambda i: (i, 0))],
        out_shape=[hm(8, BF16), hm(8, BF16), hm(4, F32), hm(12, BF16), hm(12, BF16),
                   jax.ShapeDtypeStruct((t, 256), F32)],
        compiler_params=_cparams(("parallel",)),
        name="in_projection",
    )(h2d, gain, w_perm, cos_t, sin_t, bias)


def _gelu_tanh(x):
    c = np.float32(np.sqrt(2.0 / np.pi))
    return 0.5 * x * (1.0 + jnp.tanh(c * (x + 0.044715 * (x * x * x))))


def _compress_kernel(ch_ref, pos_ref, w1_ref, w2_ref, out_ref):
    ch = ch_ref[0, 0]
    half = CMP_STRIDE * HEAD_DIM
    top = _dot((ch + pos_ref[0, 0:1, :]).astype(BF16), w1_ref[0, :half, :])
    bot = _dot((ch + pos_ref[0, 1:2, :]).astype(BF16), w1_ref[0, half:, :])
    n = ch.shape[0]
    hidden = top + pltpu.roll(bot, n - 1, 0)
    act = _gelu_tanh(hidden).astype(BF16)
    out_ref[0, 0] = _dot(act, w2_ref[0]).astype(out_ref.dtype)


def _compress(cmp4, pos, w1, w2, b, s):
    n_chunk = s // CMP_STRIDE
    width = CMP_STRIDE * HEAD_DIM
    chunks = cmp4.reshape(4, b, n_chunk, width)
    return pl.pallas_call(
        _compress_kernel,
        grid=(4, b),
        in_specs=[pl.BlockSpec((1, 1, n_chunk, width), lambda w, i: (w, i, 0, 0)),
                  pl.BlockSpec((1, 2, width), lambda w, i: (w // 2, 0, 0)),
                  pl.BlockSpec((1, 2 * width, CMP_HIDDEN), lambda w, i: (w // 2, 0, 0)),
                  pl.BlockSpec((1, CMP_HIDDEN, HEAD_DIM), lambda w, i: (w // 2, 0, 0))],
        out_specs=pl.BlockSpec((1, 1, n_chunk, HEAD_DIM), lambda w, i: (w, i, 0, 0)),
        out_shape=jax.ShapeDtypeStruct((4, b, n_chunk, HEAD_DIM), BF16),
        compiler_params=_cparams(("parallel", "parallel")),
        name="nsa_compress",
    )(chunks, pos, w1, w2)


def _split3(x):
    h1 = x.astype(BF16)
    r1 = x - h1.astype(F32)
    h2 = r1.astype(BF16)
    h3 = (r1 - h2.astype(F32)).astype(BF16)
    return h1, h2, h3


def _cumsum_kernel(lf_ref, out_ref):
    nh, r, _ = lf_ref.shape[1:]
    row = lax.broadcasted_iota(jnp.int32, (LANES, LANES), 0)
    col = lax.broadcasted_iota(jnp.int32, (LANES, LANES), 1)
    upper = jnp.where(row <= col, 1.0, 0.0).astype(BF16)
    rr = lax.broadcasted_iota(jnp.int32, (r, r), 0)
    rc = lax.broadcasted_iota(jnp.int32, (r, r), 1)
    strict = jnp.where(rc < rr, 1.0, 0.0).astype(BF16)
    for h in range(nh):
        x = lf_ref[0, h]
        within = sum(_dot(part, upper) for part in _split3(x))
        tot = jnp.broadcast_to(within[:, LANES - 1:LANES], (r, LANES))
        before = sum(_dot(strict, part) for part in _split3(tot))
        out_ref[0, h] = -(within + before)


def _neg_cumsum(lf):
    b, nh, r, _ = lf.shape
    spec = pl.BlockSpec((1, nh, r, LANES), lambda i: (i, 0, 0, 0))
    return pl.pallas_call(
        _cumsum_kernel, grid=(b,), in_specs=[spec], out_specs=spec,
        out_shape=jax.ShapeDtypeStruct(lf.shape, F32),
        compiler_params=_cparams(("parallel",)),
        name="forget_cumsum",
    )(lf)


def _nsa_cmp_kernel(q_ref, kc_ref, vc_ref, ovt_ref, oc_ref, selb_ref, *, tq, n_sel):
    q0 = pl.program_id(2) * tq
    q4 = q_ref[:, 0].reshape(NSA_HPG * tq, HEAD_DIM)
    kc = kc_ref[0, 0]
    vc = vc_ref[0, 0]
    n_pad = kc.shape[0]
    n_blk = ovt_ref.shape[0]
    s = _dot_nt(q4, kc).reshape(NSA_HPG, tq, n_pad)
    t = q0 + lax.broadcasted_iota(jnp.int32, (1, tq, 1), 1)
    n = lax.broadcasted_iota(jnp.int32, (1, 1, n_pad), 2)
    mask = (n * CMP_STRIDE + (CMP_LEN - 1) <= t) & (n < n_pad - 1)
    l = jnp.where(mask, s, NEG_INF)
    m = jnp.max(l, axis=-1, keepdims=True)
    e = jnp.where(mask, jnp.exp(l - m), 0.0)
    p = e / jnp.maximum(jnp.sum(e, axis=-1, keepdims=True), TINY)
    pb = p.astype(BF16).reshape(NSA_HPG * tq, n_pad)
    oc_ref[:, 0] = _dot(pb, vc).reshape(NSA_HPG, tq, HEAD_DIM)
    imp4 = _dot_nt(ovt_ref[...], pb)
    imp = sum(imp4[:, hh * tq:(hh + 1) * tq] for hh in range(NSA_HPG))
    t2 = q0 + lax.broadcasted_iota(jnp.int32, (1, tq), 1)
    blk = lax.broadcasted_iota(jnp.int32, (n_blk, tq), 0)
    cur = t2 // SLC_LEN
    forced = (blk == 0) | (blk == cur) | (blk == cur - 1)
    score = jnp.where(forced, FORCE, jnp.where(blk <= cur, imp, -FORCE))

    sub = lax.broadcasted_iota(jnp.int32, (8, tq), 0)
    groups = [score[8 * v:8 * v + 8] for v in range(n_blk // 8)]
    rank = [jnp.zeros((8, tq), F32) for _ in groups]
    for i in range(n_blk):
        row = jnp.broadcast_to(score[i:i + 1], (8, tq))
        for v, grp in enumerate(groups):
            if v > i // 8:
                beats = jnp.where(row >= grp, 1.0, 0.0)
            elif v < i // 8:
                beats = jnp.where(row > grp, 1.0, 0.0)
            else:
                beats = jnp.where(sub > i % 8, jnp.where(row >= grp, 1.0, 0.0),
                                  jnp.where(row > grp, 1.0, 0.0))
            rank[v] = rank[v] + beats
    sel_bias = jnp.where(jnp.concatenate(rank, axis=0) < float(n_sel), 0.0, NEG_INF)
    selb_ref[0, 0] = sel_bias.T.astype(selb_ref.dtype)


def _nsa_cmp(qn, comp, overlap, b, s, tq):
    n_pad = comp.shape[2]
    n_blk = overlap.shape[0]
    grid = (b, NSA_GROUPS, s // tq)
    kern = functools.partial(_nsa_cmp_kernel, tq=tq, n_sel=min(TOP_N, n_blk))
    return pl.pallas_call(
        kern, grid=grid,
        in_specs=[pl.BlockSpec((NSA_HPG, 1, tq, HEAD_DIM), lambda i, g, j: (g, i, j, 0)),
                  pl.BlockSpec((1, 1, n_pad, HEAD_DIM), lambda i, g, j: (g, i, 0, 0)),
                  pl.BlockSpec((1, 1, n_pad, HEAD_DIM), lambda i, g, j: (2 + g, i, 0, 0)),
                  _const_spec((n_blk, n_pad))],
        out_specs=[pl.BlockSpec((NSA_HPG, 1, tq, HEAD_DIM), lambda i, g, j: (g, i, j, 0)),
                   pl.BlockSpec((1, 1, tq, n_blk), lambda i, g, j: (i, g, j, 0))],
        out_shape=[jax.ShapeDtypeStruct((NSA_HEADS, b, s, HEAD_DIM), F32),
                   jax.ShapeDtypeStruct((b, NSA_GROUPS, s, n_blk), BF16)],
        compiler_params=_cparams(("parallel", "parallel", "parallel")),
        name="nsa_compressed_select",
    )(qn, comp, comp, overlap)


def _head_rms(o, gain):
    return o * lax.rsqrt(jnp.mean(o * o, axis=-1, keepdims=True) + RMS_EPS) * gain


def _nsa_main_kernel(q_ref, ks_ref, kw_ref, vs_ref, vw_ref, selb_ref, oc_ref, gate_ref, gain_ref,
                     out_ref, *, tq, tk):
    q0 = pl.program_id(2) * tq
    hq = NSA_HPG * tq
    q4 = q_ref[:, 0].reshape(hq, HEAD_DIM)
    selb = selb_ref[0, 0]
    t = q0 + lax.broadcasted_iota(jnp.int32, (tq, 1), 0)
    q_sel = jnp.concatenate(
        [jnp.concatenate([q_ref[hh, 0], selb], axis=-1) for hh in range(NSA_HPG)], axis=0)

    def slc_scores(j):
        start = pl.multiple_of(j * tk, tk)
        return _dot_nt(q_sel, ks_ref[0, 0, pl.ds(start, tk), :])

    def slc_tile(j, sc, carry, causal):
        m, l, acc = carry
        start = pl.multiple_of(j * tk, tk)
        v = vs_ref[0, 0, pl.ds(start, tk), :]
        sc = sc.reshape(NSA_HPG, tq, tk)
        if causal:
            kpos = start + lax.broadcasted_iota(jnp.int32, (1, tk), 1)
            sc = sc + jnp.where(kpos <= t, 0.0, NEG_INF)[None]
        m_new = jnp.maximum(m, jnp.max(sc, axis=-1, keepdims=True))
        alpha = jnp.exp(m - m_new)
        p = jnp.exp(sc - m_new)
        l = alpha * l + jnp.sum(p, axis=-1, keepdims=True)
        pv = _dot(p.astype(BF16).reshape(hq, tk), v).reshape(NSA_HPG, tq, HEAD_DIM)
        return m_new, l, alpha * acc + pv

    init = (jnp.full((NSA_HPG, tq, 1), NEG_INF, F32), jnp.zeros((NSA_HPG, tq, 1), F32),
            jnp.zeros((NSA_HPG, tq, HEAD_DIM), F32))
    n_full = q0 // tk

    def slc_step(j, carry):
        nxt = slc_scores(j + 1)
        return (nxt,) + slc_tile(j, carry[0], carry[1:], False)

    carry = lax.fori_loop(0, n_full, slc_step, (slc_scores(0),) + init)
    _, l_s, acc_s = slc_tile(n_full, carry[0], carry[1:], True)
    o_s = acc_s / l_s

    span = WINDOW + tq
    w0 = pl.multiple_of(jnp.maximum(q0 - WINDOW, 0), tq)
    kwin = kw_ref[0, 0, pl.ds(w0, span), :]
    vwin = vw_ref[0, 0, pl.ds(w0, span), :]
    kpos = w0 + lax.broadcasted_iota(jnp.int32, (1, span), 1)
    wbias = jnp.where((kpos <= t) & (t - kpos < WINDOW), 0.0, NEG_INF)
    sw = _dot_nt(q4, kwin).reshape(NSA_HPG, tq, span) + wbias[None]
    mw = jnp.max(sw, axis=-1, keepdims=True)
    ew = jnp.exp(sw - mw)
    lw = jnp.sum(ew, axis=-1, keepdims=True)
    o_w = _dot(ew.astype(BF16).reshape(hq, span), vwin).reshape(NSA_HPG, tq, HEAD_DIM) / lw

    gates = gate_ref[...]
    heads = []
    for hh in range(NSA_HPG):
        g0 = gates[:, 3 * hh:3 * hh + 1]
        g1 = gates[:, 3 * hh + 1:3 * hh + 2]
        g2 = gates[:, 3 * hh + 2:3 * hh + 3]
        o = g0 * oc_ref[hh, 0] + g1 * o_s[hh] + g2 * o_w[hh]
        heads.append(_head_rms(o, gain_ref[hh]).astype(out_ref.dtype))
    out_ref[...] = jnp.concatenate(heads, axis=-1)


def _nsa_main(qn, ks_sel, kvn, selb, oc, misc, gain16, b, s, tq, tk):
    nq = s // tq
    n_blk = selb.shape[-1]
    grid = (b, NSA_GROUPS, nq)
    kv_spec = lambda base: pl.BlockSpec((1, 1, s, HEAD_DIM), lambda i, g, j: (base + g, i, 0, 0))
    kern = functools.partial(_nsa_main_kernel, tq=tq, tk=tk)
    return pl.pallas_call(
        kern, grid=grid,
        in_specs=[pl.BlockSpec((NSA_HPG, 1, tq, HEAD_DIM), lambda i, g, j: (g, i, j, 0)),
                  pl.BlockSpec((1, 1, s, HEAD_DIM + n_blk), lambda i, g, j: (g, i, 0, 0)),
                  kv_spec(2), kv_spec(4), kv_spec(6),
                  pl.BlockSpec((1, 1, tq, n_blk), lambda i, g, j: (i, g, j, 0)),
                  pl.BlockSpec((NSA_HPG, 1, tq, HEAD_DIM), lambda i, g, j: (g, i, j, 0)),
                  pl.BlockSpec((tq, LANES), lambda i, g, j: (i * nq + j, g)),
                  pl.BlockSpec((NSA_HPG, 1, HEAD_DIM), lambda i, g, j: (g, 0, 0))],
        out_specs=pl.BlockSpec((tq, NSA_HPG * HEAD_DIM), lambda i, g, j: (i * nq + j, g)),
        out_shape=jax.ShapeDtypeStruct((b * s, NSA_HEADS * HEAD_DIM), BF16),
        compiler_params=_cparams(("parallel", "parallel", "parallel")),
        name="nsa_selected_window",
    )(qn, ks_sel, kvn, kvn, kvn, selb, oc, misc, gain16)


def _fox_kernel(q_ref, k_ref, v_ref, nc_ref, gain_ref, out_ref, o_scr, *, tq):
    i = pl.program_id(1)
    q0 = i * tq
    tk = tq
    row = q0 + lax.broadcasted_iota(jnp.int32, (tq, 1), 0)

    def head(h, _):
        q = q_ref[h, 0]

        def tile(j, carry, causal):
            m, l, acc = carry
            start = pl.multiple_of(j * tk, tk)
            k = k_ref[h, 0, pl.ds(start, tk), :]
            v = v_ref[h, 0, pl.ds(start, tk), :]
            sc = _dot_nt(q, k) + nc_ref[0, h, j]
            if causal:
                kpos = start + lax.broadcasted_iota(jnp.int32, (1, tk), 1)
                sc = jnp.where(kpos <= row, sc, NEG_INF)
            m_new = jnp.maximum(m, jnp.max(sc, axis=-1, keepdims=True))
            alpha = jnp.exp(m - m_new)
            p = jnp.exp(sc - m_new)
            l = alpha * l + jnp.sum(p, axis=-1, keepdims=True)
            return m_new, l, alpha * acc + _dot(p.astype(BF16), v)

        init = (jnp.full((tq, 1), NEG_INF, F32), jnp.zeros((tq, 1), F32),
                jnp.zeros((tq, HEAD_DIM), F32))
        carry = lax.fori_loop(0, i, lambda j, c: tile(j, c, False), init)
        _, l, acc = tile(i, carry, True)
        o_scr[h] = _head_rms(acc / l, gain_ref[h])
        return 0

    lax.fori_loop(0, FOX_HEADS, head, 0)
    out_ref[...] = jnp.concatenate([o_scr[h] for h in range(FOX_HEADS)],
                                   axis=-1).astype(out_ref.dtype)


def _fox(fox, negc, gain16, b, s, tq):
    nq = s // tq
    kern = functools.partial(_fox_kernel, tq=tq)
    return pl.pallas_call(
        kern, grid=(b, nq),
        in_specs=[pl.BlockSpec((FOX_HEADS, 1, tq, HEAD_DIM), lambda i, j: (0, i, j, 0)),
                  pl.BlockSpec((FOX_HEADS, 1, s, HEAD_DIM), lambda i, j: (1, i, 0, 0)),
                  pl.BlockSpec((FOX_HEADS, 1, s, HEAD_DIM), lambda i, j: (2, i, 0, 0)),
                  pl.BlockSpec((1, FOX_HEADS, nq, 1, tq), lambda i, j: (i, 0, 0, 0, 0)),
                  pl.BlockSpec((FOX_HEADS, 1, HEAD_DIM), lambda i, j: (2, 0, 0))],
        out_specs=pl.BlockSpec((tq, FOX_HEADS * HEAD_DIM), lambda i, j: (i * nq + j, 0)),
        out_shape=jax.ShapeDtypeStruct((b * s, FOX_HEADS * HEAD_DIM), BF16),
        scratch_shapes=[pltpu.VMEM((FOX_HEADS, tq, HEAD_DIM), F32)],
        compiler_params=_cparams(("parallel", "parallel")),
        name="fox_attention",
    )(fox, fox, fox, negc, gain16)


def _sb_kernel(q_ref, k_ref, v_ref, gain_ref, out_ref, acc_ref, *, tq, tk):
    i = pl.program_id(1)
    q0 = i * tq
    row = q0 + lax.broadcasted_iota(jnp.int32, (tq, 1), 0)
    r_i = lax.broadcasted_iota(jnp.int32, (tk, tk), 0)
    c_i = lax.broadcasted_iota(jnp.int32, (tk, tk), 1)
    suffix = jnp.where(r_i >= c_i, 1.0, 0.0).astype(BF16)
    assert tq == tk
    heads = range(SB_HEADS)
    qs = [q_ref[h, 0] for h in heads]
    acc_ref[...] = jnp.zeros_like(acc_ref)

    def scores(h, j):
        start = pl.multiple_of(jnp.maximum(j, 0) * tk, tk)
        return _dot_nt(qs[h], k_ref[h, 0, pl.ds(start, tk), :])

    def tile(h, j, z, tail, masked):
        start = pl.multiple_of(j * tk, tk)
        v = v_ref[h, 0, pl.ds(start, tk), :]
        lr = -(jnp.maximum(z, 0.0) + jnp.log(1.0 + jnp.exp(-jnp.abs(z))))
        if masked:
            mask = (start + lax.broadcasted_iota(jnp.int32, (1, tk), 1)) < row
            lr = jnp.where(mask, lr, 0.0)
        hi = lr.astype(BF16)
        lo = (lr - hi.astype(F32)).astype(BF16)
        both = _dot(jnp.concatenate([hi, lo], axis=0), suffix)
        cs = both[:tq] + both[tq:]
        a = jnp.exp(z + cs + tail)
        if masked:
            a = jnp.where(mask, a, 0.0)
        acc_ref[h] += _dot(a.astype(BF16), v)
        return tail + cs[:, 0:1]

    zs = tuple(scores(h, i) for h in heads)
    nxt = tuple(scores(h, i - 1) for h in heads)
    tails = tuple(tile(h, i, zs[h], jnp.zeros((tq, 1), F32), True) for h in heads)

    def step(jj, carry):
        zs, tails = carry
        j = i - 1 - jj
        nxt = tuple(scores(h, j - 1) for h in heads)
        return nxt, tuple(tile(h, j, zs[h], tails[h], False) for h in heads)

    lax.fori_loop(0, i, step, (nxt, tails))
    out_ref[...] = jnp.concatenate([_head_rms(acc_ref[h], gain_ref[h]) for h in heads],
                                   axis=-1).astype(out_ref.dtype)


def _stick_breaking(sb, gain16, b, s, tq, tk):
    nq = s // tq
    kern = functools.partial(_sb_kernel, tq=tq, tk=tk)
    return pl.pallas_call(
        kern, grid=(b, nq),
        in_specs=[pl.BlockSpec((SB_HEADS, 1, tq, HEAD_DIM), lambda i, j: (0, i, j, 0)),
                  pl.BlockSpec((SB_HEADS, 1, s, HEAD_DIM), lambda i, j: (1, i, 0, 0)),
                  pl.BlockSpec((SB_HEADS, 1, s, HEAD_DIM), lambda i, j: (2, i, 0, 0)),
                  pl.BlockSpec((SB_HEADS, 1, HEAD_DIM), lambda i, j: (3, 0, 0))],
        out_specs=pl.BlockSpec((tq, SB_HEADS * HEAD_DIM), lambda i, j: (i * nq + j, 0)),
        out_shape=jax.ShapeDtypeStruct((b * s, SB_HEADS * HEAD_DIM), BF16),
        scratch_shapes=[pltpu.VMEM((SB_HEADS, tq, HEAD_DIM), F32)],
        compiler_params=_cparams(("parallel", "parallel")),
        name="stick_breaking_attention",
    )(sb, sb, sb, gain16)


def _outproj_kernel(h_ref, on_ref, of_ref, os_ref, w_ref, out_ref):
    n1 = on_ref.shape[1]
    n2 = n1 + of_ref.shape[1]
    mix = (_dot(on_ref[...], w_ref[:n1, :]) + _dot(of_ref[...], w_ref[n1:n2, :])
           + _dot(os_ref[...], w_ref[n2:, :]))
    out_ref[...] = h_ref[...] + mix


def _out_projection(h2d, o_nsa, o_fox, o_sb, w, tm):
    t, d = h2d.shape
    row = lambda n: pl.BlockSpec((tm, n), lambda i: (i, 0))
    return pl.pallas_call(
        _outproj_kernel, grid=(t // tm,),
        in_specs=[row(d), row(o_nsa.shape[1]), row(o_fox.shape[1]), row(o_sb.shape[1]),
                  _const_spec(w.shape)],
        out_specs=row(d),
        out_shape=jax.ShapeDtypeStruct((t, d), F32),
        compiler_params=_cparams(("parallel",)),
        name="out_projection",
    )(h2d, o_nsa, o_fox, o_sb, w)


def _rms_rows(x, gain):
    return x * lax.rsqrt(jnp.mean(x * x, axis=-1, keepdims=True) + RMS_EPS) * gain


def _ffn_kernel(h_ref, g_ref, wg_ref, wu_ref, wd_ref, out_ref, *, tf):
    h = h_ref[...]
    hn = _rms_rows(h, g_ref[...]).astype(BF16)
    acc = h
    for c in range(wg_ref.shape[1] // tf):
        sl = slice(c * tf, (c + 1) * tf)
        gate = _dot(hn, wg_ref[:, sl])
        up = _dot(hn, wu_ref[:, sl])
        act = (gate * jax.nn.sigmoid(gate) * up).astype(BF16)
        acc = acc + _dot(act, wd_ref[sl, :])
    out_ref[...] = acc


def _ffn(h2d, gain, wg, wu, wd, tm, tf):
    t, d = h2d.shape
    row = pl.BlockSpec((tm, d), lambda i: (i, 0))
    return pl.pallas_call(
        functools.partial(_ffn_kernel, tf=tf), grid=(t // tm,),
        in_specs=[row, _const_spec((1, d)), _const_spec(wg.shape), _const_spec(wu.shape),
                  _const_spec(wd.shape)],
        out_specs=row,
        out_shape=jax.ShapeDtypeStruct((t, d), F32),
        compiler_params=_cparams(("parallel",)),
        name="swiglu_ffn",
    )(h2d, gain, wg, wu, wd)


def _ple_kernel(h_ref, p_ref, g_ref, wp_ref, wg_ref, gf_ref, out_ref, *, final):
    h = h_ref[...]
    hn = _rms_rows(h, g_ref[...]).astype(BF16)
    gate = jax.nn.sigmoid(_dot(hn, wg_ref[...]))
    new = h + _dot(p_ref[...].astype(BF16), wp_ref[...]) * gate
    if final:
        new = _rms_rows(new, gf_ref[...])
    out_ref[...] = new


def _ple(h2d, p2d, gain, wp, wg, gain_final, final, tm):
    t, d = h2d.shape
    row = lambda n: pl.BlockSpec((tm, n), lambda i: (i, 0))
    return pl.pallas_call(
        functools.partial(_ple_kernel, final=final), grid=(t // tm,),
        in_specs=[row(d), row(p2d.shape[1]), _const_spec((1, d)), _const_spec(wp.shape),
                  _const_spec(wg.shape), _const_spec((1, d))],
        out_specs=row(d),
        out_shape=jax.ShapeDtypeStruct((t, d), F32),
        compiler_params=_cparams(("parallel",)),
        name="per_layer_embedding",
    )(h2d, p2d, gain, wp, wg, gain_final)


def _overlap_matrix(n_pad, n_blk):
    n_cmp = n_pad - 1
    cmp_start = np.arange(n_pad) * CMP_STRIDE
    blk_start = np.arange(n_blk) * SLC_LEN
    ov = ((cmp_start[:, None] < blk_start[None, :] + SLC_LEN)
          & (cmp_start[:, None] + CMP_LEN > blk_start[None, :])
          & (np.arange(n_pad)[:, None] < n_cmp))
    return jnp.asarray(ov.T.astype(np.float32), dtype=BF16)


def _pick(n, pref):
    return pref if n % pref == 0 else n


def kernel(x, p, positions, norm_mix, w_in, b_nsa_gate, b_forget, nsa_cmp_pos_k, nsa_cmp_w1_k, nsa_cmp_w2_k, nsa_cmp_pos_v, nsa_cmp_w1_v, nsa_cmp_w2_v, head_norm, w_out, norm_ffn, w_ffn_gate, w_ffn_up, w_ffn_down, norm_ple, w_ple_proj, w_ple_gate, norm_final):
    b, s, d = x.shape
    depth = w_in.shape[0]
    t = b * s
    assert s % 512 == 0 and s >= WINDOW + 128
    tm = _pick(t, 512)
    tq_nsa, tk_nsa = 128, 256
    tq_cmp = 256
    tq_fox = 512
    tq_sb, tk_sb = 256, 256
    n_chunk = s // CMP_STRIDE
    n_blk = s // SLC_LEN

    half = HEAD_DIM // 2
    inv_freq = ROPE_THETA ** (-jnp.arange(half, dtype=F32) / half)
    ang = positions.astype(F32)[..., None] * inv_freq
    cos, sin = jnp.cos(ang), jnp.sin(ang)
    cos_t = jnp.concatenate([cos] * 4, axis=-1).reshape(t, LANES)
    sin_t = jnp.concatenate([-sin, sin, -sin, sin], axis=-1).reshape(t, LANES)
    overlap = _overlap_matrix(n_chunk, n_blk)
    blk_onehot = jnp.asarray(
        (np.arange(s)[:, None] // SLC_LEN == np.arange(n_blk)[None, :]).astype(np.float32), dtype=BF16)

    h = x.reshape(t, d)
    for i in range(depth):
        w_perm = _permute_w_in(w_in[i]).astype(BF16)
        bias = _misc_bias(b_nsa_gate[i], b_forget[i])
        qn, kvn, cmp4, fox, sb, misc = _in_projection(
            h, norm_mix[i].reshape(1, d), w_perm, cos_t, sin_t, bias, tm)

        width = CMP_STRIDE * HEAD_DIM
        pos = jnp.stack([nsa_cmp_pos_k[i], nsa_cmp_pos_v[i]]).reshape(2, 2, width)
        w1 = jnp.stack([nsa_cmp_w1_k[i], nsa_cmp_w1_v[i]]).astype(BF16)
        w2 = jnp.stack([nsa_cmp_w2_k[i], nsa_cmp_w2_v[i]]).astype(BF16)
        comp = _compress(cmp4, pos, w1, w2, b, s)

        gain16 = head_norm[i].reshape(N_HEADS, 1, HEAD_DIM)
        qn4 = qn.reshape(NSA_HEADS, b, s, HEAD_DIM)
        oc, selb = _nsa_cmp(qn4, comp, overlap, b, s, tq_cmp)
        kvn4 = kvn.reshape(8, b, s, HEAD_DIM)
        ks_sel = jnp.concatenate(
            [kvn4[:NSA_GROUPS], jnp.broadcast_to(blk_onehot, (NSA_GROUPS, b, s, n_blk))], axis=-1)
        o_nsa = _nsa_main(qn4, ks_sel, kvn4, selb, oc, misc, gain16, b, s, tq_nsa, tk_nsa)

        log_f = misc[:, 12:16].reshape(b, s // LANES, LANES, FOX_HEADS).transpose(0, 3, 1, 2)
        negc = _neg_cumsum(log_f).reshape(b, FOX_HEADS, s // tq_fox, 1, tq_fox)
        o_fox = _fox(fox.reshape(12, b, s, HEAD_DIM), negc, gain16, b, s, tq_fox)
        o_sb = _stick_breaking(sb.reshape(12, b, s, HEAD_DIM), gain16, b, s, tq_sb, tk_sb)

        h = _out_projection(h, o_nsa, o_fox, o_sb, w_out[i].astype(BF16), tm)
        h = _ffn(h, norm_ffn[i].reshape(1, d), w_ffn_gate[i].astype(BF16),
                 w_ffn_up[i].astype(BF16), w_ffn_down[i].astype(BF16), tm, 256)
        h = _ple(h, p[i].reshape(t, -1), norm_ple[i].reshape(1, d), w_ple_proj[i].astype(BF16),
                 w_ple_gate[i].astype(BF16), norm_final.reshape(1, d), i == depth - 1, tm)
    return h.reshape(b, s, d)
```

```python
import functools

import numpy as np
import jax
import jax.numpy as jnp
from jax import lax
from jax.experimental import pallas as pl
from jax.experimental.pallas import tpu as pltpu

F32 = jnp.float32
BF16 = jnp.bfloat16

HEAD_DIM = 64
NSA_HEADS = 8
NSA_GROUPS = 2
NSA_HPG = 4
FOX_HEADS = 4
SB_HEADS = 4
N_HEADS = 16
CMP_LEN = 32
CMP_STRIDE = 16
CMP_HIDDEN = 128
SLC_LEN = 64
TOP_N = 16
WINDOW = 512
ROPE_THETA = 10000.0
RMS_EPS = 1e-6
NEG_INF = -1e30
FORCE = 1e9
TINY = 1e-20
Q_SCALE = HEAD_DIM ** -0.5

LANES = 128
IN_PAD = 3072
VMEM_LIMIT = 56 * 1024 * 1024

_OFF = dict(nq=0, nkc=512, nvc=640, nks=768, nvs=896, nkw=1024, nvw=1152, ngate=1280,
            fq=1304, fk=1560, fv=1816, ff=2072, sq=2076, sk=2332, sv=2588)


def _dot(a, b):
    return jnp.dot(a, b, preferred_element_type=F32)


def _dot_nt(a, b):
    return lax.dot_general(a, b, (((1,), (1,)), ((), ())), preferred_element_type=F32)


def _cparams(sem):
    return pltpu.CompilerParams(dimension_semantics=sem, vmem_limit_bytes=VMEM_LIMIT)


def _const_spec(shape):
    nd = len(shape)
    return pl.BlockSpec(shape, lambda *_: (0,) * nd, pipeline_mode=pl.Buffered(1))


def _permute_w_in(w):
    d = w.shape[0]

    def cols(name, width):
        return w[:, _OFF[name]:_OFF[name] + width]

    gate = cols("ngate", 24)
    zeros = lambda n: jnp.zeros((d, n), w.dtype)
    parts = [cols("nq", 512), cols("nkc", 128), cols("nks", 128), cols("nkw", 128),
             cols("nvc", 128), cols("nvs", 128), cols("nvw", 128),
             cols("fq", 256), cols("fk", 256), cols("fv", 256),
             cols("sq", 256), cols("sk", 256), cols("sv", 256),
             gate[:, :12], cols("ff", 4), zeros(112), gate[:, 12:], zeros(116)]
    return jnp.concatenate(parts, axis=1)


def _misc_bias(b_gate, b_forget):
    z = lambda n: jnp.zeros((n,), F32)
    return jnp.concatenate([b_gate[:12], b_forget, z(112), b_gate[12:], z(116)]).reshape(1, 256)


def _rope_half(x, cos, sin_signed):
    lane = lax.broadcasted_iota(jnp.int32, x.shape, 1)
    first = (lane % HEAD_DIM) < (HEAD_DIM // 2)
    partner = jnp.where(first, pltpu.roll(x, LANES - 32, 1), pltpu.roll(x, 32, 1))
    return x * cos + partner * sin_signed


def _log_sigmoid(x):
    return jnp.minimum(x, 0.0) - jnp.log(1.0 + jnp.exp(-jnp.abs(x)))


def _inproj_kernel(x_ref, g_ref, w_ref, cos_ref, sin_ref, bias_ref,
                   qn_ref, kvn_ref, cmp_ref, fox_ref, sb_ref, misc_ref):
    x = x_ref[...]
    ms = jnp.mean(x * x, axis=-1, keepdims=True)
    hn = (x * lax.rsqrt(ms + RMS_EPS) * g_ref[...]).astype(BF16)
    cos = cos_ref[...]
    sin = sin_ref[...]

    plan = [(qn_ref, 0, True, True), (qn_ref, 2, True, True),
            (qn_ref, 4, True, True), (qn_ref, 6, True, True),
            (cmp_ref, 0, True, False), (kvn_ref, 0, True, False),
            (kvn_ref, 2, True, False), (cmp_ref, 2, False, False),
            (kvn_ref, 4, False, False), (kvn_ref, 6, False, False),
            (fox_ref, 0, False, True), (fox_ref, 2, False, True),
            (fox_ref, 4, False, False), (fox_ref, 6, False, False),
            (fox_ref, 8, False, False), (fox_ref, 10, False, False),
            (sb_ref, 0, False, True), (sb_ref, 2, False, True),
            (sb_ref, 4, False, False), (sb_ref, 6, False, False),
            (sb_ref, 8, False, False), (sb_ref, 10, False, False)]
    for c in range(IN_PAD // 256):
        res = _dot(hn, w_ref[:, c * 256:(c + 1) * 256])
        if c == IN_PAD // 256 - 1:
            z = res + bias_ref[...]
            lane = lax.broadcasted_iota(jnp.int32, z.shape, 1)
            is_forget = (lane >= 12) & (lane < 16)
            misc_ref[...] = jnp.where(is_forget, _log_sigmoid(z), jax.nn.sigmoid(z))
            continue
        for half in range(2):
            dst, slot, rot, scl = plan[2 * c + half]
            y = res[:, half * LANES:(half + 1) * LANES]
            if rot:
                y = _rope_half(y, cos, sin)
            if scl:
                y = y * Q_SCALE
            y = y.astype(dst.dtype)
            dst[slot] = y[:, :HEAD_DIM]
            dst[slot + 1] = y[:, HEAD_DIM:]


def _in_projection(h2d, gain, w_perm, cos_t, sin_t, bias, tm):
    t, d = h2d.shape
    grid = (t // tm,)
    hm = lambda n, dt: jax.ShapeDtypeStruct((n, t, HEAD_DIM), dt)
    hm_spec = lambda n: pl.BlockSpec((n, tm, HEAD_DIM), lambda i: (0, i, 0))
    return pl.pallas_call(
        _inproj_kernel,
        grid=grid,
        in_specs=[pl.BlockSpec((tm, d), lambda i: (i, 0)),
                  _const_spec((1, d)),
                  _const_spec((d, IN_PAD)),
                  pl.BlockSpec((tm, LANES), lambda i: (i, 0)),
                  pl.BlockSpec((tm, LANES), lambda i: (i, 0)),
                  _const_spec((1, 256))],
        out_specs=[hm_spec(8), hm_spec(8), hm_spec(4), hm_spec(12), hm_spec(12),
                   pl.BlockSpec((tm, 256), lambda i: (i, 0))],
        out_shape=[hm(8, BF16), hm(8, BF16), hm(4, F32), hm(12, BF16), hm(12, BF16),
                   jax.ShapeDtypeStruct((t, 256), F32)],
        compiler_params=_cparams(("parallel",)),
        name="in_projection",
    )(h2d, gain, w_perm, cos_t, sin_t, bias)


def _gelu_tanh(x):
    c = np.float32(np.sqrt(2.0 / np.pi))
    return 0.5 * x * (1.0 + jnp.tanh(c * (x + 0.044715 * (x * x * x))))


def _compress_kernel(ch_ref, pos_ref, w1_ref, w2_ref, out_ref):
    ch = ch_ref[0, 0]
    half = CMP_STRIDE * HEAD_DIM
    top = _dot((ch + pos_ref[0, 0:1, :]).astype(BF16), w1_ref[0, :half, :])
    bot = _dot((ch + pos_ref[0, 1:2, :]).astype(BF16), w1_ref[0, half:, :])
    n = ch.shape[0]
    hidden = top + pltpu.roll(bot, n - 1, 0)
    act = _gelu_tanh(hidden).astype(BF16)
    out_ref[0, 0] = _dot(act, w2_ref[0]).astype(out_ref.dtype)


def _compress(cmp4, pos, w1, w2, b, s):
    n_chunk = s // CMP_STRIDE
    width = CMP_STRIDE * HEAD_DIM
    chunks = cmp4.reshape(4, b, n_chunk, width)
    return pl.pallas_call(
        _compress_kernel,
        grid=(4, b),
        in_specs=[pl.BlockSpec((1, 1, n_chunk, width), lambda w, i: (w, i, 0, 0)),
                  pl.BlockSpec((1, 2, width), lambda w, i: (w // 2, 0, 0)),
                  pl.BlockSpec((1, 2 * width, CMP_HIDDEN), lambda w, i: (w // 2, 0, 0)),
                  pl.BlockSpec((1, CMP_HIDDEN, HEAD_DIM), lambda w, i: (w // 2, 0, 0))],
        out_specs=pl.BlockSpec((1, 1, n_chunk, HEAD_DIM), lambda w, i: (w, i, 0, 0)),
        out_shape=jax.ShapeDtypeStruct((4, b, n_chunk, HEAD_DIM), BF16),
        compiler_params=_cparams(("parallel", "parallel")),
        name="nsa_compress",
    )(chunks, pos, w1, w2)


def _split3(x):
    h1 = x.astype(BF16)
    r1 = x - h1.astype(F32)
    h2 = r1.astype(BF16)
    h3 = (r1 - h2.astype(F32)).astype(BF16)
    return h1, h2, h3


def _cumsum_kernel(lf_ref, out_ref):
    nh, r, _ = lf_ref.shape[1:]
    row = lax.broadcasted_iota(jnp.int32, (LANES, LANES), 0)
    col = lax.broadcasted_iota(jnp.int32, (LANES, LANES), 1)
    upper = jnp.where(row <= col, 1.0, 0.0).astype(BF16)
    rr = lax.broadcasted_iota(jnp.int32, (r, r), 0)
    rc = lax.broadcasted_iota(jnp.int32, (r, r), 1)
    strict = jnp.where(rc < rr, 1.0, 0.0).astype(BF16)
    for h in range(nh):
        x = lf_ref[0, h]
        within = sum(_dot(part, upper) for part in _split3(x))
        tot = jnp.broadcast_to(within[:, LANES - 1:LANES], (r, LANES))
        before = sum(_dot(strict, part) for part in _split3(tot))
        out_ref[0, h] = -(within + before)


def _neg_cumsum(lf):
    b, nh, r, _ = lf.shape
    spec = pl.BlockSpec((1, nh, r, LANES), lambda i: (i, 0, 0, 0))
    return pl.pallas_call(
        _cumsum_kernel, grid=(b,), in_specs=[spec], out_specs=spec,
        out_shape=jax.ShapeDtypeStruct(lf.shape, F32),
        compiler_params=_cparams(("parallel",)),
        name="forget_cumsum",
    )(lf)


def _nsa_cmp_kernel(q_ref, kc_ref, vc_ref, ovt_ref, oc_ref, selb_ref, *, tq, n_sel):
    q0 = pl.program_id(2) * tq
    q4 = q_ref[:, 0].reshape(NSA_HPG * tq, HEAD_DIM)
    kc = kc_ref[0, 0]
    vc = vc_ref[0, 0]
    n_pad = kc.shape[0]
    n_blk = ovt_ref.shape[0]
    s = _dot_nt(q4, kc).reshape(NSA_HPG, tq, n_pad)
    t = q0 + lax.broadcasted_iota(jnp.int32, (1, tq, 1), 1)
    n = lax.broadcasted_iota(jnp.int32, (1, 1, n_pad), 2)
    mask = (n * CMP_STRIDE + (CMP_LEN - 1) <= t) & (n < n_pad - 1)
    l = jnp.where(mask, s, NEG_INF)
    m = jnp.max(l, axis=-1, keepdims=True)
    e = jnp.where(mask, jnp.exp(l - m), 0.0)
    p = e / jnp.maximum(jnp.sum(e, axis=-1, keepdims=True), TINY)
    pb = p.astype(BF16).reshape(NSA_HPG * tq, n_pad)
    oc_ref[:, 0] = _dot(pb, vc).reshape(NSA_HPG, tq, HEAD_DIM)
    imp4 = _dot_nt(ovt_ref[...], pb)
    imp = sum(imp4[:, hh * tq:(hh + 1) * tq] for hh in range(NSA_HPG))
    t2 = q0 + lax.broadcasted_iota(jnp.int32, (1, tq), 1)
    blk = lax.broadcasted_iota(jnp.int32, (n_blk, tq), 0)
    cur = t2 // SLC_LEN
    forced = (blk == 0) | (blk == cur) | (blk == cur - 1)
    score = jnp.where(forced, FORCE, jnp.where(blk <= cur, imp, -FORCE))

    sub = lax.broadcasted_iota(jnp.int32, (8, tq), 0)
    groups = [score[8 * v:8 * v + 8] for v in range(n_blk // 8)]
    rank = [jnp.zeros((8, tq), F32) for _ in groups]
    for i in range(n_blk):
        row = jnp.broadcast_to(score[i:i + 1], (8, tq))
        for v, grp in enumerate(groups):
            if v > i // 8:
                beats = jnp.where(row >= grp, 1.0, 0.0)
            elif v < i // 8:
                beats = jnp.where(row > grp, 1.0, 0.0)
            else:
                beats = jnp.where(sub > i % 8, jnp.where(row >= grp, 1.0, 0.0),
                                  jnp.where(row > grp, 1.0, 0.0))
            rank[v] = rank[v] + beats
    sel_bias = jnp.where(jnp.concatenate(rank, axis=0) < float(n_sel), 0.0, NEG_INF)
    selb_ref[0, 0] = sel_bias.T.astype(selb_ref.dtype)


def _nsa_cmp(qn, comp, overlap, b, s, tq):
    n_pad = comp.shape[2]
    n_blk = overlap.shape[0]
    grid = (b, NSA_GROUPS, s // tq)
    kern = functools.partial(_nsa_cmp_kernel, tq=tq, n_sel=min(TOP_N, n_blk))
    return pl.pallas_call(
        kern, grid=grid,
        in_specs=[pl.BlockSpec((NSA_HPG, 1, tq, HEAD_DIM), lambda i, g, j: (g, i, j, 0)),
                  pl.BlockSpec((1, 1, n_pad, HEAD_DIM), lambda i, g, j: (g, i, 0, 0)),
                  pl.BlockSpec((1, 1, n_pad, HEAD_DIM), lambda i, g, j: (2 + g, i, 0, 0)),
                  _const_spec((n_blk, n_pad))],
        out_specs=[pl.BlockSpec((NSA_HPG, 1, tq, HEAD_DIM), lambda i, g, j: (g, i, j, 0)),
                   pl.BlockSpec((1, 1, tq, n_blk), lambda i, g, j: (i, g, j, 0))],
        out_shape=[jax.ShapeDtypeStruct((NSA_HEADS, b, s, HEAD_DIM), F32),
                   jax.ShapeDtypeStruct((b, NSA_GROUPS, s, n_blk), BF16)],
        compiler_params=_cparams(("parallel", "parallel", "parallel")),
        name="nsa_compressed_select",
    )(qn, comp, comp, overlap)


def _head_rms(o, gain):
    return o * lax.rsqrt(jnp.mean(o * o, axis=-1, keepdims=True) + RMS_EPS) * gain


def _nsa_main_kernel(q_ref, ks_ref, kw_ref, vst_ref, vwt_ref, selb_ref, oc_ref, gate_ref, gain_ref,
                     out_ref, *, tq, tk):
    q0 = pl.program_id(2) * tq
    hq = NSA_HPG * tq
    q4 = q_ref[:, 0].reshape(hq, HEAD_DIM)
    selb = selb_ref[0, 0]
    q_sel = jnp.concatenate(
        [jnp.concatenate([q_ref[hh, 0], selb], axis=-1) for hh in range(NSA_HPG)], axis=0)
    t = q0 + lax.broadcasted_iota(jnp.int32, (1, tq), 1)

    def per_head(bias):
        return jnp.concatenate([bias] * NSA_HPG, axis=1)

    def slc_scores(j):
        start = pl.multiple_of(j * tk, tk)
        return _dot_nt(ks_ref[0, 0, pl.ds(start, tk), :], q_sel)

    def slc_tile(j, sc, carry, causal):
        m, l, acc = carry
        start = pl.multiple_of(j * tk, tk)
        if causal:
            kpos = start + lax.broadcasted_iota(jnp.int32, (tk, 1), 0)
            sc = sc + per_head(jnp.where(kpos <= t, 0.0, NEG_INF))
        m_new = jnp.maximum(m, jnp.max(sc, axis=0, keepdims=True))
        alpha = jnp.exp(m - m_new)
        p = jnp.exp(sc - m_new)
        l = alpha * l + jnp.sum(p, axis=0, keepdims=True)
        pv = _dot(vst_ref[0, 0, :, pl.ds(start, tk)], p.astype(BF16))
        return m_new, l, alpha * acc + pv

    init = (jnp.full((1, hq), NEG_INF, F32), jnp.zeros((1, hq), F32),
            jnp.zeros((HEAD_DIM, hq), F32))
    n_full = q0 // tk
    def slc_step(j, carry):
        nxt = slc_scores(j + 1)
        return (nxt,) + slc_tile(j, carry[0], carry[1:], False)

    carry = lax.fori_loop(0, n_full, slc_step, (slc_scores(0),) + init)
    _, l_s, acc_s = slc_tile(n_full, carry[0], carry[1:], True)
    o_s = (acc_s / l_s).T

    span = WINDOW + tq
    w0 = pl.multiple_of(jnp.maximum(q0 - WINDOW, 0), tq)
    kpos = w0 + lax.broadcasted_iota(jnp.int32, (span, 1), 0)
    wbias = jnp.where((kpos <= t) & (t - kpos < WINDOW), 0.0, NEG_INF)
    sw = _dot_nt(kw_ref[0, 0, pl.ds(w0, span), :], q4) + per_head(wbias)
    mw = jnp.max(sw, axis=0, keepdims=True)
    ew = jnp.exp(sw - mw)
    lw = jnp.sum(ew, axis=0, keepdims=True)
    o_w = (_dot(vwt_ref[0, 0, :, pl.ds(w0, span)], ew.astype(BF16)) / lw).T

    gates = gate_ref[...]
    heads = []
    for hh in range(NSA_HPG):
        rows = slice(hh * tq, (hh + 1) * tq)
        g0 = gates[:, 3 * hh:3 * hh + 1]
        g1 = gates[:, 3 * hh + 1:3 * hh + 2]
        g2 = gates[:, 3 * hh + 2:3 * hh + 3]
        o = g0 * oc_ref[hh, 0] + g1 * o_s[rows] + g2 * o_w[rows]
        heads.append(_head_rms(o, gain_ref[hh]).astype(out_ref.dtype))
    out_ref[...] = jnp.concatenate(heads, axis=-1)


def _nsa_main(qn, ks_sel, kvn, v_t, selb, oc, misc, gain16, b, s, tq, tk):
    nq = s // tq
    n_blk = selb.shape[-1]
    grid = (b, NSA_GROUPS, nq)
    vt_spec = lambda base: pl.BlockSpec((1, 1, HEAD_DIM, s), lambda i, g, j: (base + g, i, 0, 0))
    kern = functools.partial(_nsa_main_kernel, tq=tq, tk=tk)
    return pl.pallas_call(
        kern, grid=grid,
        in_specs=[pl.BlockSpec((NSA_HPG, 1, tq, HEAD_DIM), lambda i, g, j: (g, i, j, 0)),
                  pl.BlockSpec((1, 1, s, HEAD_DIM + n_blk), lambda i, g, j: (g, i, 0, 0)),
                  pl.BlockSpec((1, 1, s, HEAD_DIM), lambda i, g, j: (2 + g, i, 0, 0)),
                  vt_spec(0), vt_spec(2),
                  pl.BlockSpec((1, 1, tq, n_blk), lambda i, g, j: (i, g, j, 0)),
                  pl.BlockSpec((NSA_HPG, 1, tq, HEAD_DIM), lambda i, g, j: (g, i, j, 0)),
                  pl.BlockSpec((tq, LANES), lambda i, g, j: (i * nq + j, g)),
                  pl.BlockSpec((NSA_HPG, 1, HEAD_DIM), lambda i, g, j: (g, 0, 0))],
        out_specs=pl.BlockSpec((tq, NSA_HPG * HEAD_DIM), lambda i, g, j: (i * nq + j, g)),
        out_shape=jax.ShapeDtypeStruct((b * s, NSA_HEADS * HEAD_DIM), BF16),
        compiler_params=_cparams(("parallel", "parallel", "parallel")),
        name="nsa_selected_window",
    )(qn, ks_sel, kvn, v_t, v_t, selb, oc, misc, gain16)


def _fox_kernel(q_ref, k_ref, v_ref, nc_ref, gain_ref, out_ref, o_scr, *, tq):
    i = pl.program_id(1)
    q0 = i * tq
    tk = tq
    row = q0 + lax.broadcasted_iota(jnp.int32, (tq, 1), 0)

    def head(h, _):
        q = q_ref[h, 0]

        def tile(j, carry, causal):
            m, l, acc = carry
            start = pl.multiple_of(j * tk, tk)
            k = k_ref[h, 0, pl.ds(start, tk), :]
            v = v_ref[h, 0, pl.ds(start, tk), :]
            sc = _dot_nt(q, k) + nc_ref[0, h, j]
            if causal:
                kpos = start + lax.broadcasted_iota(jnp.int32, (1, tk), 1)
                sc = jnp.where(kpos <= row, sc, NEG_INF)
            m_new = jnp.maximum(m, jnp.max(sc, axis=-1, keepdims=True))
            alpha = jnp.exp(m - m_new)
            p = jnp.exp(sc - m_new)
            l = alpha * l + jnp.sum(p, axis=-1, keepdims=True)
            return m_new, l, alpha * acc + _dot(p.astype(BF16), v)

        init = (jnp.full((tq, 1), NEG_INF, F32), jnp.zeros((tq, 1), F32),
                jnp.zeros((tq, HEAD_DIM), F32))
        carry = lax.fori_loop(0, i, lambda j, c: tile(j, c, False), init)
        _, l, acc = tile(i, carry, True)
        o_scr[h] = _head_rms(acc / l, gain_ref[h])
        return 0

    lax.fori_loop(0, FOX_HEADS, head, 0)
    out_ref[...] = jnp.concatenate([o_scr[h] for h in range(FOX_HEADS)],
                                   axis=-1).astype(out_ref.dtype)


def _fox(fox, negc, gain16, b, s, tq):
    nq = s // tq
    kern = functools.partial(_fox_kernel, tq=tq)
    return pl.pallas_call(
        kern, grid=(b, nq),
        in_specs=[pl.BlockSpec((FOX_HEADS, 1, tq, HEAD_DIM), lambda i, j: (0, i, j, 0)),
                  pl.BlockSpec((FOX_HEADS, 1, s, HEAD_DIM), lambda i, j: (1, i, 0, 0)),
                  pl.BlockSpec((FOX_HEADS, 1, s, HEAD_DIM), lambda i, j: (2, i, 0, 0)),
                  pl.BlockSpec((1, FOX_HEADS, nq, 1, tq), lambda i, j: (i, 0, 0, 0, 0)),
                  pl.BlockSpec((FOX_HEADS, 1, HEAD_DIM), lambda i, j: (2, 0, 0))],
        out_specs=pl.BlockSpec((tq, FOX_HEADS * HEAD_DIM), lambda i, j: (i * nq + j, 0)),
        out_shape=jax.ShapeDtypeStruct((b * s, FOX_HEADS * HEAD_DIM), BF16),
        scratch_shapes=[pltpu.VMEM((FOX_HEADS, tq, HEAD_DIM), F32)],
        compiler_params=_cparams(("parallel", "parallel")),
        name="fox_attention",
    )(fox, fox, fox, negc, gain16)


def _sb_kernel(q_ref, k_ref, v_ref, gain_ref, out_ref, acc_ref, *, tq, tk):
    i = pl.program_id(1)
    q0 = i * tq
    row = q0 + lax.broadcasted_iota(jnp.int32, (tq, 1), 0)
    r_i = lax.broadcasted_iota(jnp.int32, (tk, tk), 0)
    c_i = lax.broadcasted_iota(jnp.int32, (tk, tk), 1)
    suffix = jnp.where(r_i >= c_i, 1.0, 0.0).astype(BF16)
    assert tq == tk
    heads = range(SB_HEADS)
    qs = [q_ref[h, 0] for h in heads]
    acc_ref[...] = jnp.zeros_like(acc_ref)

    def scores(h, j):
        start = pl.multiple_of(jnp.maximum(j, 0) * tk, tk)
        return _dot_nt(qs[h], k_ref[h, 0, pl.ds(start, tk), :])

    def tile(h, j, z, tail, masked):
        start = pl.multiple_of(j * tk, tk)
        v = v_ref[h, 0, pl.ds(start, tk), :]
        lr = -(jnp.maximum(z, 0.0) + jnp.log(1.0 + jnp.exp(-jnp.abs(z))))
        if masked:
            mask = (start + lax.broadcasted_iota(jnp.int32, (1, tk), 1)) < row
            lr = jnp.where(mask, lr, 0.0)
        hi = lr.astype(BF16)
        lo = (lr - hi.astype(F32)).astype(BF16)
        both = _dot(jnp.concatenate([hi, lo], axis=0), suffix)
        cs = both[:tq] + both[tq:]
        a = jnp.exp(z + cs + tail)
        if masked:
            a = jnp.where(mask, a, 0.0)
        acc_ref[h] += _dot(a.astype(BF16), v)
        return tail + cs[:, 0:1]

    zs = tuple(scores(h, i) for h in heads)
    nxt = tuple(scores(h, i - 1) for h in heads)
    tails = tuple(tile(h, i, zs[h], jnp.zeros((tq, 1), F32), True) for h in heads)

    def step(jj, carry):
        zs, tails = carry
        j = i - 1 - jj
        nxt = tuple(scores(h, j - 1) for h in heads)
        return nxt, tuple(tile(h, j, zs[h], tails[h], False) for h in heads)

    lax.fori_loop(0, i, step, (nxt, tails))
    out_ref[...] = jnp.concatenate([_head_rms(acc_ref[h], gain_ref[h]) for h in heads],
                                   axis=-1).astype(out_ref.dtype)


def _stick_breaking(sb, gain16, b, s, tq, tk):
    nq = s // tq
    kern = functools.partial(_sb_kernel, tq=tq, tk=tk)
    return pl.pallas_call(
        kern, grid=(b, nq),
        in_specs=[pl.BlockSpec((SB_HEADS, 1, tq, HEAD_DIM), lambda i, j: (0, i, j, 0)),
                  pl.BlockSpec((SB_HEADS, 1, s, HEAD_DIM), lambda i, j: (1, i, 0, 0)),
                  pl.BlockSpec((SB_HEADS, 1, s, HEAD_DIM), lambda i, j: (2, i, 0, 0)),
                  pl.BlockSpec((SB_HEADS, 1, HEAD_DIM), lambda i, j: (3, 0, 0))],
        out_specs=pl.BlockSpec((tq, SB_HEADS * HEAD_DIM), lambda i, j: (i * nq + j, 0)),
        out_shape=jax.ShapeDtypeStruct((b * s, SB_HEADS * HEAD_DIM), BF16),
        scratch_shapes=[pltpu.VMEM((SB_HEADS, tq, HEAD_DIM), F32)],
        compiler_params=_cparams(("parallel", "parallel")),
        name="stick_breaking_attention",
    )(sb, sb, sb, gain16)


def _outproj_kernel(h_ref, on_ref, of_ref, os_ref, w_ref, out_ref):
    n1 = on_ref.shape[1]
    n2 = n1 + of_ref.shape[1]
    mix = (_dot(on_ref[...], w_ref[:n1, :]) + _dot(of_ref[...], w_ref[n1:n2, :])
           + _dot(os_ref[...], w_ref[n2:, :]))
    out_ref[...] = h_ref[...] + mix


def _out_projection(h2d, o_nsa, o_fox, o_sb, w, tm):
    t, d = h2d.shape
    row = lambda n: pl.BlockSpec((tm, n), lambda i: (i, 0))
    return pl.pallas_call(
        _outproj_kernel, grid=(t // tm,),
        in_specs=[row(d), row(o_nsa.shape[1]), row(o_fox.shape[1]), row(o_sb.shape[1]),
                  _const_spec(w.shape)],
        out_specs=row(d),
        out_shape=jax.ShapeDtypeStruct((t, d), F32),
        compiler_params=_cparams(("parallel",)),
        name="out_projection",
    )(h2d, o_nsa, o_fox, o_sb, w)


def _rms_rows(x, gain):
    return x * lax.rsqrt(jnp.mean(x * x, axis=-1, keepdims=True) + RMS_EPS) * gain


def _ffn_kernel(h_ref, g_ref, wg_ref, wu_ref, wd_ref, out_ref, *, tf):
    h = h_ref[...]
    hn = _rms_rows(h, g_ref[...]).astype(BF16)
    acc = h
    for c in range(wg_ref.shape[1] // tf):
        sl = slice(c * tf, (c + 1) * tf)
        gate = _dot(hn, wg_ref[:, sl])
        up = _dot(hn, wu_ref[:, sl])
        act = (gate * jax.nn.sigmoid(gate) * up).astype(BF16)
        acc = acc + _dot(act, wd_ref[sl, :])
    out_ref[...] = acc


def _ffn(h2d, gain, wg, wu, wd, tm, tf):
    t, d = h2d.shape
    row = pl.BlockSpec((tm, d), lambda i: (i, 0))
    return pl.pallas_call(
        functools.partial(_ffn_kernel, tf=tf), grid=(t // tm,),
        in_specs=[row, _const_spec((1, d)), _const_spec(wg.shape), _const_spec(wu.shape),
                  _const_spec(wd.shape)],
        out_specs=row,
        out_shape=jax.ShapeDtypeStruct((t, d), F32),
        compiler_params=_cparams(("parallel",)),
        name="swiglu_ffn",
    )(h2d, gain, wg, wu, wd)


def _ple_kernel(h_ref, p_ref, g_ref, wp_ref, wg_ref, gf_ref, out_ref, *, final):
    h = h_ref[...]
    hn = _rms_rows(h, g_ref[...]).astype(BF16)
    gate = jax.nn.sigmoid(_dot(hn, wg_ref[...]))
    new = h + _dot(p_ref[...].astype(BF16), wp_ref[...]) * gate
    if final:
        new = _rms_rows(new, gf_ref[...])
    out_ref[...] = new


def _ple(h2d, p2d, gain, wp, wg, gain_final, final, tm):
    t, d = h2d.shape
    row = lambda n: pl.BlockSpec((tm, n), lambda i: (i, 0))
    return pl.pallas_call(
        functools.partial(_ple_kernel, final=final), grid=(t // tm,),
        in_specs=[row(d), row(p2d.shape[1]), _const_spec((1, d)), _const_spec(wp.shape),
                  _const_spec(wg.shape), _const_spec((1, d))],
        out_specs=row(d),
        out_shape=jax.ShapeDtypeStruct((t, d), F32),
        compiler_params=_cparams(("parallel",)),
        name="per_layer_embedding",
    )(h2d, p2d, gain, wp, wg, gain_final)


def _overlap_matrix(n_pad, n_blk):
    n_cmp = n_pad - 1
    cmp_start = np.arange(n_pad) * CMP_STRIDE
    blk_start = np.arange(n_blk) * SLC_LEN
    ov = ((cmp_start[:, None] < blk_start[None, :] + SLC_LEN)
          & (cmp_start[:, None] + CMP_LEN > blk_start[None, :])
          & (np.arange(n_pad)[:, None] < n_cmp))
    return jnp.asarray(ov.T.astype(np.float32), dtype=BF16)


def _pick(n, pref):
    return pref if n % pref == 0 else n


def kernel(x, p, positions, norm_mix, w_in, b_nsa_gate, b_forget, nsa_cmp_pos_k, nsa_cmp_w1_k, nsa_cmp_w2_k, nsa_cmp_pos_v, nsa_cmp_w1_v, nsa_cmp_w2_v, head_norm, w_out, norm_ffn, w_ffn_gate, w_ffn_up, w_ffn_down, norm_ple, w_ple_proj, w_ple_gate, norm_final):
    b, s, d = x.shape
    depth = w_in.shape[0]
    t = b * s
    assert s % 512 == 0 and s >= WINDOW + 128
    tm = _pick(t, 512)
    tq_nsa, tk_nsa = 256, 256
    tq_cmp = 256
    tq_fox = 512
    tq_sb, tk_sb = 256, 256
    n_chunk = s // CMP_STRIDE
    n_blk = s // SLC_LEN

    half = HEAD_DIM // 2
    inv_freq = ROPE_THETA ** (-jnp.arange(half, dtype=F32) / half)
    ang = positions.astype(F32)[..., None] * inv_freq
    cos, sin = jnp.cos(ang), jnp.sin(ang)
    cos_t = jnp.concatenate([cos] * 4, axis=-1).reshape(t, LANES)
    sin_t = jnp.concatenate([-sin, sin, -sin, sin], axis=-1).reshape(t, LANES)
    overlap = _overlap_matrix(n_chunk, n_blk)
    blk_onehot = jnp.asarray(
        (np.arange(s)[:, None] // SLC_LEN == np.arange(n_blk)[None, :]).astype(np.float32), dtype=BF16)

    h = x.reshape(t, d)
    for i in range(depth):
        w_perm = _permute_w_in(w_in[i]).astype(BF16)
        bias = _misc_bias(b_nsa_gate[i], b_forget[i])
        qn, kvn, cmp4, fox, sb, misc = _in_projection(
            h, norm_mix[i].reshape(1, d), w_perm, cos_t, sin_t, bias, tm)

        width = CMP_STRIDE * HEAD_DIM
        pos = jnp.stack([nsa_cmp_pos_k[i], nsa_cmp_pos_v[i]]).reshape(2, 2, width)
        w1 = jnp.stack([nsa_cmp_w1_k[i], nsa_cmp_w1_v[i]]).astype(BF16)
        w2 = jnp.stack([nsa_cmp_w2_k[i], nsa_cmp_w2_v[i]]).astype(BF16)
        comp = _compress(cmp4, pos, w1, w2, b, s)

        gain16 = head_norm[i].reshape(N_HEADS, 1, HEAD_DIM)
        qn4 = qn.reshape(NSA_HEADS, b, s, HEAD_DIM)
        oc, selb = _nsa_cmp(qn4, comp, overlap, b, s, tq_cmp)
        kvn4 = kvn.reshape(8, b, s, HEAD_DIM)
        ks_sel = jnp.concatenate(
            [kvn4[:NSA_GROUPS], jnp.broadcast_to(blk_onehot, (NSA_GROUPS, b, s, n_blk))], axis=-1)
        v_t = jnp.swapaxes(kvn4[4:], 2, 3)
        o_nsa = _nsa_main(qn4, ks_sel, kvn4, v_t, selb, oc, misc, gain16, b, s, tq_nsa, tk_nsa)

        log_f = misc[:, 12:16].reshape(b, s // LANES, LANES, FOX_HEADS).transpose(0, 3, 1, 2)
        negc = _neg_cumsum(log_f).reshape(b, FOX_HEADS, s // tq_fox, 1, tq_fox)
        o_fox = _fox(fox.reshape(12, b, s, HEAD_DIM), negc, gain16, b, s, tq_fox)
        o_sb = _stick_breaking(sb.reshape(12, b, s, HEAD_DIM), gain16, b, s, tq_sb, tk_sb)

        h = _out_projection(h, o_nsa, o_fox, o_sb, w_out[i].astype(BF16), tm)
        h = _ffn(h, norm_ffn[i].reshape(1, d), w_ffn_gate[i].astype(BF16),
                 w_ffn_up[i].astype(BF16), w_ffn_down[i].astype(BF16), tm, 256)
        h = _ple(h, p[i].reshape(t, -1), norm_ple[i].reshape(1, d), w_ple_proj[i].astype(BF16),
                 w_ple_gate[i].astype(BF16), norm_final.reshape(1, d), i == depth - 1, tm)
    return h.reshape(b, s, d)
```

```python
import functools

import numpy as np
import jax
import jax.numpy as jnp
from jax import lax
from jax.experimental import pallas as pl
from jax.experimental.pallas import tpu as pltpu

F32 = jnp.float32
BF16 = jnp.bfloat16

HEAD_DIM = 64
NSA_HEADS = 8
NSA_GROUPS = 2
NSA_HPG = 4
FOX_HEADS = 4
SB_HEADS = 4
N_HEADS = 16
CMP_LEN = 32
CMP_STRIDE = 16
CMP_HIDDEN = 128
SLC_LEN = 64
TOP_N = 16
WINDOW = 512
ROPE_THETA = 10000.0
RMS_EPS = 1e-6
NEG_INF = -1e30
FORCE = 1e9
TINY = 1e-20
Q_SCALE = HEAD_DIM ** -0.5

LANES = 128
IN_PAD = 3072
VMEM_LIMIT = 56 * 1024 * 1024

_OFF = dict(nq=0, nkc=512, nvc=640, nks=768, nvs=896, nkw=1024, nvw=1152, ngate=1280,
            fq=1304, fk=1560, fv=1816, ff=2072, sq=2076, sk=2332, sv=2588)


def _dot(a, b):
    return jnp.dot(a, b, preferred_element_type=F32)


def _dot_nt(a, b):
    return lax.dot_general(a, b, (((1,), (1,)), ((), ())), preferred_element_type=F32)


def _cparams(sem):
    return pltpu.CompilerParams(dimension_semantics=sem, vmem_limit_bytes=VMEM_LIMIT)


def _const_spec(shape):
    nd = len(shape)
    return pl.BlockSpec(shape, lambda *_: (0,) * nd, pipeline_mode=pl.Buffered(1))


def _permute_w_in(w):
    d = w.shape[0]

    def cols(name, width):
        return w[:, _OFF[name]:_OFF[name] + width]

    gate = cols("ngate", 24)
    zeros = lambda n: jnp.zeros((d, n), w.dtype)
    parts = [cols("nq", 512), cols("nkc", 128), cols("nks", 128), cols("nkw", 128),
             cols("nvc", 128), cols("nvs", 128), cols("nvw", 128),
             cols("fq", 256), cols("fk", 256), cols("fv", 256),
             cols("sq", 256), cols("sk", 256), cols("sv", 256),
             gate[:, :12], cols("ff", 4), zeros(112), gate[:, 12:], zeros(116)]
    return jnp.concatenate(parts, axis=1)


def _misc_bias(b_gate, b_forget):
    z = lambda n: jnp.zeros((n,), F32)
    return jnp.concatenate([b_gate[:12], b_forget, z(112), b_gate[12:], z(116)]).reshape(1, 256)


def _rope_half(x, cos, sin_signed):
    lane = lax.broadcasted_iota(jnp.int32, x.shape, 1)
    first = (lane % HEAD_DIM) < (HEAD_DIM // 2)
    partner = jnp.where(first, pltpu.roll(x, LANES - 32, 1), pltpu.roll(x, 32, 1))
    return x * cos + partner * sin_signed


def _log_sigmoid(x):
    return jnp.minimum(x, 0.0) - jnp.log(1.0 + jnp.exp(-jnp.abs(x)))


def _inproj_kernel(x_ref, g_ref, w_ref, cos_ref, sin_ref, bias_ref,
                   qn_ref, kvn_ref, cmp_ref, fox_ref, sb_ref, misc_ref):
    x = x_ref[...]
    ms = jnp.mean(x * x, axis=-1, keepdims=True)
    hn = (x * lax.rsqrt(ms + RMS_EPS) * g_ref[...]).astype(BF16)
    cos = cos_ref[...]
    sin = sin_ref[...]

    plan = [(qn_ref, 0, True, True), (qn_ref, 2, True, True),
            (qn_ref, 4, True, True), (qn_ref, 6, True, True),
            (cmp_ref, 0, True, False), (kvn_ref, 0, True, False),
            (kvn_ref, 2, True, False), (cmp_ref, 2, False, False),
            (kvn_ref, 4, False, False), (kvn_ref, 6, False, False),
            (fox_ref, 0, False, True), (fox_ref, 2, False, True),
            (fox_ref, 4, False, False), (fox_ref, 6, False, False),
            (fox_ref, 8, False, False), (fox_ref, 10, False, False),
            (sb_ref, 0, False, True), (sb_ref, 2, False, True),
            (sb_ref, 4, False, False), (sb_ref, 6, False, False),
            (sb_ref, 8, False, False), (sb_ref, 10, False, False)]
    for c in range(IN_PAD // 256):
        res = _dot(hn, w_ref[:, c * 256:(c + 1) * 256])
        if c == IN_PAD // 256 - 1:
            z = res + bias_ref[...]
            lane = lax.broadcasted_iota(jnp.int32, z.shape, 1)
            is_forget = (lane >= 12) & (lane < 16)
            misc_ref[...] = jnp.where(is_forget, _log_sigmoid(z), jax.nn.sigmoid(z))
            continue
        for half in range(2):
            dst, slot, rot, scl = plan[2 * c + half]
            y = res[:, half * LANES:(half + 1) * LANES]
            if rot:
                y = _rope_half(y, cos, sin)
            if scl:
                y = y * Q_SCALE
            y = y.astype(dst.dtype)
            dst[slot] = y[:, :HEAD_DIM]
            dst[slot + 1] = y[:, HEAD_DIM:]


def _in_projection(h2d, gain, w_perm, cos_t, sin_t, bias, tm):
    t, d = h2d.shape
    grid = (t // tm,)
    hm = lambda n, dt: jax.ShapeDtypeStruct((n, t, HEAD_DIM), dt)
    hm_spec = lambda n: pl.BlockSpec((n, tm, HEAD_DIM), lambda i: (0, i, 0))
    return pl.pallas_call(
        _inproj_kernel,
        grid=grid,
        in_specs=[pl.BlockSpec((tm, d), lambda i: (i, 0)),
                  _const_spec((1, d)),
                  _const_spec((d, IN_PAD)),
                  pl.BlockSpec((tm, LANES), lambda i: (i, 0)),
                  pl.BlockSpec((tm, LANES), lambda i: (i, 0)),
                  _const_spec((1, 256))],
        out_specs=[hm_spec(8), hm_spec(8), hm_spec(4), hm_spec(12), hm_spec(12),
                   pl.BlockSpec((tm, 256), lambda i: (i, 0))],
        out_shape=[hm(8, BF16), hm(8, BF16), hm(4, F32), hm(12, BF16), hm(12, BF16),
                   jax.ShapeDtypeStruct((t, 256), F32)],
        compiler_params=_cparams(("parallel",)),
        name="in_projection",
    )(h2d, gain, w_perm, cos_t, sin_t, bias)


def _gelu_tanh(x):
    c = np.float32(np.sqrt(2.0 / np.pi))
    return 0.5 * x * (1.0 + jnp.tanh(c * (x + 0.044715 * (x * x * x))))


def _compress_kernel(ch_ref, pos_ref, w1_ref, w2_ref, out_ref):
    ch = ch_ref[0, 0]
    half = CMP_STRIDE * HEAD_DIM
    top = _dot((ch + pos_ref[0, 0:1, :]).astype(BF16), w1_ref[0, :half, :])
    bot = _dot((ch + pos_ref[0, 1:2, :]).astype(BF16), w1_ref[0, half:, :])
    n = ch.shape[0]
    hidden = top + pltpu.roll(bot, n - 1, 0)
    act = _gelu_tanh(hidden).astype(BF16)
    out_ref[0, 0] = _dot(act, w2_ref[0]).astype(out_ref.dtype)


def _compress(cmp4, pos, w1, w2, b, s):
    n_chunk = s // CMP_STRIDE
    width = CMP_STRIDE * HEAD_DIM
    chunks = cmp4.reshape(4, b, n_chunk, width)
    return pl.pallas_call(
        _compress_kernel,
        grid=(4, b),
        in_specs=[pl.BlockSpec((1, 1, n_chunk, width), lambda w, i: (w, i, 0, 0)),
                  pl.BlockSpec((1, 2, width), lambda w, i: (w // 2, 0, 0)),
                  pl.BlockSpec((1, 2 * width, CMP_HIDDEN), lambda w, i: (w // 2, 0, 0)),
                  pl.BlockSpec((1, CMP_HIDDEN, HEAD_DIM), lambda w, i: (w // 2, 0, 0))],
        out_specs=pl.BlockSpec((1, 1, n_chunk, HEAD_DIM), lambda w, i: (w, i, 0, 0)),
        out_shape=jax.ShapeDtypeStruct((4, b, n_chunk, HEAD_DIM), BF16),
        compiler_params=_cparams(("parallel", "parallel")),
        name="nsa_compress",
    )(chunks, pos, w1, w2)


def _split3(x):
    h1 = x.astype(BF16)
    r1 = x - h1.astype(F32)
    h2 = r1.astype(BF16)
    h3 = (r1 - h2.astype(F32)).astype(BF16)
    return h1, h2, h3


def _cumsum_kernel(lf_ref, out_ref):
    nh, r, _ = lf_ref.shape[1:]
    row = lax.broadcasted_iota(jnp.int32, (LANES, LANES), 0)
    col = lax.broadcasted_iota(jnp.int32, (LANES, LANES), 1)
    upper = jnp.where(row <= col, 1.0, 0.0).astype(BF16)
    rr = lax.broadcasted_iota(jnp.int32, (r, r), 0)
    rc = lax.broadcasted_iota(jnp.int32, (r, r), 1)
    strict = jnp.where(rc < rr, 1.0, 0.0).astype(BF16)
    for h in range(nh):
        x = lf_ref[0, h]
        within = sum(_dot(part, upper) for part in _split3(x))
        tot = jnp.broadcast_to(within[:, LANES - 1:LANES], (r, LANES))
        before = sum(_dot(strict, part) for part in _split3(tot))
        out_ref[0, h] = -(within + before)


def _neg_cumsum(lf):
    b, nh, r, _ = lf.shape
    spec = pl.BlockSpec((1, nh, r, LANES), lambda i: (i, 0, 0, 0))
    return pl.pallas_call(
        _cumsum_kernel, grid=(b,), in_specs=[spec], out_specs=spec,
        out_shape=jax.ShapeDtypeStruct(lf.shape, F32),
        compiler_params=_cparams(("parallel",)),
        name="forget_cumsum",
    )(lf)


def _nsa_cmp_kernel(q_ref, kc_ref, vc_ref, ovt_ref, oc_ref, selb_ref, *, tq, n_sel):
    q0 = pl.program_id(2) * tq
    q4 = q_ref[:, 0].reshape(NSA_HPG * tq, HEAD_DIM)
    kc = kc_ref[0, 0]
    vc = vc_ref[0, 0]
    n_pad = kc.shape[0]
    n_blk = ovt_ref.shape[0]
    s = _dot_nt(q4, kc).reshape(NSA_HPG, tq, n_pad)
    t = q0 + lax.broadcasted_iota(jnp.int32, (1, tq, 1), 1)
    n = lax.broadcasted_iota(jnp.int32, (1, 1, n_pad), 2)
    mask = (n * CMP_STRIDE + (CMP_LEN - 1) <= t) & (n < n_pad - 1)
    l = jnp.where(mask, s, NEG_INF)
    m = jnp.max(l, axis=-1, keepdims=True)
    e = jnp.where(mask, jnp.exp(l - m), 0.0)
    p = e / jnp.maximum(jnp.sum(e, axis=-1, keepdims=True), TINY)
    pb = p.astype(BF16).reshape(NSA_HPG * tq, n_pad)
    oc_ref[:, 0] = _dot(pb, vc).reshape(NSA_HPG, tq, HEAD_DIM)
    imp4 = _dot_nt(ovt_ref[...], pb)
    imp = sum(imp4[:, hh * tq:(hh + 1) * tq] for hh in range(NSA_HPG))
    t2 = q0 + lax.broadcasted_iota(jnp.int32, (1, tq), 1)
    blk = lax.broadcasted_iota(jnp.int32, (n_blk, tq), 0)
    cur = t2 // SLC_LEN
    forced = (blk == 0) | (blk == cur) | (blk == cur - 1)
    score = jnp.where(forced, FORCE, jnp.where(blk <= cur, imp, -FORCE))

    sub = lax.broadcasted_iota(jnp.int32, (8, tq), 0)
    groups = [score[8 * v:8 * v + 8] for v in range(n_blk // 8)]
    rank = [jnp.zeros((8, tq), F32) for _ in groups]
    for i in range(n_blk):
        row = jnp.broadcast_to(score[i:i + 1], (8, tq))
        for v, grp in enumerate(groups):
            if v > i // 8:
                beats = jnp.where(row >= grp, 1.0, 0.0)
            elif v < i // 8:
                beats = jnp.where(row > grp, 1.0, 0.0)
            else:
                beats = jnp.where(sub > i % 8, jnp.where(row >= grp, 1.0, 0.0),
                                  jnp.where(row > grp, 1.0, 0.0))
            rank[v] = rank[v] + beats
    sel_bias = jnp.where(jnp.concatenate(rank, axis=0) < float(n_sel), 0.0, NEG_INF)
    selb_ref[0, 0] = sel_bias.T.astype(selb_ref.dtype)


def _nsa_cmp(qn, comp, overlap, b, s, tq):
    n_pad = comp.shape[2]
    n_blk = overlap.shape[0]
    grid = (b, NSA_GROUPS, s // tq)
    kern = functools.partial(_nsa_cmp_kernel, tq=tq, n_sel=min(TOP_N, n_blk))
    return pl.pallas_call(
        kern, grid=grid,
        in_specs=[pl.BlockSpec((NSA_HPG, 1, tq, HEAD_DIM), lambda i, g, j: (g, i, j, 0)),
                  pl.BlockSpec((1, 1, n_pad, HEAD_DIM), lambda i, g, j: (g, i, 0, 0)),
                  pl.BlockSpec((1, 1, n_pad, HEAD_DIM), lambda i, g, j: (2 + g, i, 0, 0)),
                  _const_spec((n_blk, n_pad))],
        out_specs=[pl.BlockSpec((NSA_HPG, 1, tq, HEAD_DIM), lambda i, g, j: (g, i, j, 0)),
                   pl.BlockSpec((1, 1, tq, n_blk), lambda i, g, j: (i, g, j, 0))],
        out_shape=[jax.ShapeDtypeStruct((NSA_HEADS, b, s, HEAD_DIM), F32),
                   jax.ShapeDtypeStruct((b, NSA_GROUPS, s, n_blk), BF16)],
        compiler_params=_cparams(("parallel", "parallel", "parallel")),
        name="nsa_compressed_select",
    )(qn, comp, comp, overlap)


def _head_rms(o, gain):
    return o * lax.rsqrt(jnp.mean(o * o, axis=-1, keepdims=True) + RMS_EPS) * gain


def _nsa_main_kernel(q_ref, ks_ref, kw_ref, vst_ref, vwt_ref, selb_ref, oc_ref, gate_ref, gain_ref,
                     out_ref, *, tq, tk):
    q0 = pl.program_id(2) * tq
    hq = NSA_HPG * tq
    q4 = q_ref[:, 0].reshape(hq, HEAD_DIM)
    selb = selb_ref[0, 0]
    q_sel = jnp.concatenate(
        [jnp.concatenate([q_ref[hh, 0], selb], axis=-1) for hh in range(NSA_HPG)], axis=0)
    t = q0 + lax.broadcasted_iota(jnp.int32, (1, tq), 1)

    def per_head(bias):
        return jnp.concatenate([bias] * NSA_HPG, axis=1)

    def slc_scores(j):
        start = pl.multiple_of(j * tk, tk)
        return _dot_nt(ks_ref[0, 0, pl.ds(start, tk), :], q_sel)

    def slc_tile(j, sc, carry, causal):
        m, l, acc = carry
        start = pl.multiple_of(j * tk, tk)
        if causal:
            kpos = start + lax.broadcasted_iota(jnp.int32, (tk, 1), 0)
            sc = sc + per_head(jnp.where(kpos <= t, 0.0, NEG_INF))
        m_new = jnp.maximum(m, jnp.max(sc, axis=0, keepdims=True))
        alpha = jnp.exp(m - m_new)
        p = jnp.exp(sc - m_new)
        l = alpha * l + jnp.sum(p, axis=0, keepdims=True)
        pv = _dot(vst_ref[0, 0, :, pl.ds(start, tk)], p.astype(BF16))
        return m_new, l, alpha * acc + pv

    init = (jnp.full((1, hq), NEG_INF, F32), jnp.zeros((1, hq), F32),
            jnp.zeros((HEAD_DIM, hq), F32))
    n_full = q0 // tk
    def slc_step(j, carry):
        nxt = slc_scores(j + 1)
        return (nxt,) + slc_tile(j, carry[0], carry[1:], False)

    carry = lax.fori_loop(0, n_full, slc_step, (slc_scores(0),) + init)
    _, l_s, acc_s = slc_tile(n_full, carry[0], carry[1:], True)
    o_s = (acc_s / l_s).T

    span = WINDOW + tq
    w0 = pl.multiple_of(jnp.maximum(q0 - WINDOW, 0), tq)
    kpos = w0 + lax.broadcasted_iota(jnp.int32, (span, 1), 0)
    wbias = jnp.where((kpos <= t) & (t - kpos < WINDOW), 0.0, NEG_INF)
    sw = _dot_nt(kw_ref[0, 0, pl.ds(w0, span), :], q4) + per_head(wbias)
    mw = jnp.max(sw, axis=0, keepdims=True)
    ew = jnp.exp(sw - mw)
    lw = jnp.sum(ew, axis=0, keepdims=True)
    o_w = (_dot(vwt_ref[0, 0, :, pl.ds(w0, span)], ew.astype(BF16)) / lw).T

    gates = gate_ref[...]
    heads = []
    for hh in range(NSA_HPG):
        rows = slice(hh * tq, (hh + 1) * tq)
        g0 = gates[:, 3 * hh:3 * hh + 1]
        g1 = gates[:, 3 * hh + 1:3 * hh + 2]
        g2 = gates[:, 3 * hh + 2:3 * hh + 3]
        o = g0 * oc_ref[hh, 0] + g1 * o_s[rows] + g2 * o_w[rows]
        heads.append(_head_rms(o, gain_ref[hh]).astype(out_ref.dtype))
    out_ref[...] = jnp.concatenate(heads, axis=-1)


def _nsa_main(qn, ks_sel, kvn, v_t, selb, oc, misc, gain16, b, s, tq, tk):
    nq = s // tq
    n_blk = selb.shape[-1]
    grid = (b, NSA_GROUPS, nq)
    vt_spec = lambda base: pl.BlockSpec((1, 1, HEAD_DIM, s), lambda i, g, j: (base + g, i, 0, 0))
    kern = functools.partial(_nsa_main_kernel, tq=tq, tk=tk)
    return pl.pallas_call(
        kern, grid=grid,
        in_specs=[pl.BlockSpec((NSA_HPG, 1, tq, HEAD_DIM), lambda i, g, j: (g, i, j, 0)),
                  pl.BlockSpec((1, 1, s, HEAD_DIM + n_blk), lambda i, g, j: (g, i, 0, 0)),
                  pl.BlockSpec((1, 1, s, HEAD_DIM), lambda i, g, j: (2 + g, i, 0, 0)),
                  vt_spec(0), vt_spec(2),
                  pl.BlockSpec((1, 1, tq, n_blk), lambda i, g, j: (i, g, j, 0)),
                  pl.BlockSpec((NSA_HPG, 1, tq, HEAD_DIM), lambda i, g, j: (g, i, j, 0)),
                  pl.BlockSpec((tq, LANES), lambda i, g, j: (i * nq + j, g)),
                  pl.BlockSpec((NSA_HPG, 1, HEAD_DIM), lambda i, g, j: (g, 0, 0))],
        out_specs=pl.BlockSpec((tq, NSA_HPG * HEAD_DIM), lambda i, g, j: (i * nq + j, g)),
        out_shape=jax.ShapeDtypeStruct((b * s, NSA_HEADS * HEAD_DIM), BF16),
        compiler_params=_cparams(("parallel", "parallel", "parallel")),
        name="nsa_selected_window",
    )(qn, ks_sel, kvn, v_t, v_t, selb, oc, misc, gain16)


def _fox_kernel(q_ref, k_ref, v_ref, nc_ref, gain_ref, out_ref, o_scr, *, tq):
    i = pl.program_id(1)
    q0 = i * tq
    tk = tq
    row = q0 + lax.broadcasted_iota(jnp.int32, (tq, 1), 0)

    def head(h, _):
        q = q_ref[h, 0]

        def tile(j, carry, causal):
            m, l, acc = carry
            start = pl.multiple_of(j * tk, tk)
            k = k_ref[h, 0, pl.ds(start, tk), :]
            v = v_ref[h, 0, pl.ds(start, tk), :]
            sc = _dot_nt(q, k) + nc_ref[0, h, j]
            if causal:
                kpos = start + lax.broadcasted_iota(jnp.int32, (1, tk), 1)
                sc = jnp.where(kpos <= row, sc, NEG_INF)
            m_new = jnp.maximum(m, jnp.max(sc, axis=-1, keepdims=True))
            alpha = jnp.exp(m - m_new)
            p = jnp.exp(sc - m_new)
            l = alpha * l + jnp.sum(p, axis=-1, keepdims=True)
            return m_new, l, alpha * acc + _dot(p.astype(BF16), v)

        init = (jnp.full((tq, 1), NEG_INF, F32), jnp.zeros((tq, 1), F32),
                jnp.zeros((tq, HEAD_DIM), F32))
        carry = lax.fori_loop(0, i, lambda j, c: tile(j, c, False), init)
        _, l, acc = tile(i, carry, True)
        o_scr[h] = _head_rms(acc / l, gain_ref[h])
        return 0

    lax.fori_loop(0, FOX_HEADS, head, 0)
    out_ref[...] = jnp.concatenate([o_scr[h] for h in range(FOX_HEADS)],
                                   axis=-1).astype(out_ref.dtype)


def _fox(fox, negc, gain16, b, s, tq):
    nq = s // tq
    kern = functools.partial(_fox_kernel, tq=tq)
    return pl.pallas_call(
        kern, grid=(b, nq),
        in_specs=[pl.BlockSpec((FOX_HEADS, 1, tq, HEAD_DIM), lambda i, j: (0, i, j, 0)),
                  pl.BlockSpec((FOX_HEADS, 1, s, HEAD_DIM), lambda i, j: (1, i, 0, 0)),
                  pl.BlockSpec((FOX_HEADS, 1, s, HEAD_DIM), lambda i, j: (2, i, 0, 0)),
                  pl.BlockSpec((1, FOX_HEADS, nq, 1, tq), lambda i, j: (i, 0, 0, 0, 0)),
                  pl.BlockSpec((FOX_HEADS, 1, HEAD_DIM), lambda i, j: (2, 0, 0))],
        out_specs=pl.BlockSpec((tq, FOX_HEADS * HEAD_DIM), lambda i, j: (i * nq + j, 0)),
        out_shape=jax.ShapeDtypeStruct((b * s, FOX_HEADS * HEAD_DIM), BF16),
        scratch_shapes=[pltpu.VMEM((FOX_HEADS, tq, HEAD_DIM), F32)],
        compiler_params=_cparams(("parallel", "parallel")),
        name="fox_attention",
    )(fox, fox, fox, negc, gain16)


def _sb_kernel(q_ref, k_ref, v_ref, gain_ref, out_ref, acc_ref, *, tq, tk):
    i = pl.program_id(1)
    q0 = i * tq
    row = q0 + lax.broadcasted_iota(jnp.int32, (tq, 1), 0)
    r_i = lax.broadcasted_iota(jnp.int32, (tk, tk), 0)
    c_i = lax.broadcasted_iota(jnp.int32, (tk, tk), 1)
    suffix = jnp.where(r_i >= c_i, 1.0, 0.0).astype(BF16)
    assert tq == tk
    heads = range(SB_HEADS)
    qs = [q_ref[h, 0] for h in heads]
    acc_ref[...] = jnp.zeros_like(acc_ref)

    def scores(h, j):
        start = pl.multiple_of(jnp.maximum(j, 0) * tk, tk)
        return _dot_nt(qs[h], k_ref[h, 0, pl.ds(start, tk), :])

    def tile(h, j, z, tail, masked):
        start = pl.multiple_of(j * tk, tk)
        v = v_ref[h, 0, pl.ds(start, tk), :]
        lr = -(jnp.maximum(z, 0.0) + jnp.log(1.0 + jnp.exp(-jnp.abs(z))))
        if masked:
            mask = (start + lax.broadcasted_iota(jnp.int32, (1, tk), 1)) < row
            lr = jnp.where(mask, lr, 0.0)
        hi = lr.astype(BF16)
        lo = (lr - hi.astype(F32)).astype(BF16)
        both = _dot(jnp.concatenate([hi, lo], axis=0), suffix)
        cs = both[:tq] + both[tq:]
        a = jnp.exp(z + cs + tail)
        if masked:
            a = jnp.where(mask, a, 0.0)
        acc_ref[h] += _dot(a.astype(BF16), v)
        return tail + cs[:, 0:1]

    zs = tuple(scores(h, i) for h in heads)
    nxt = tuple(scores(h, i - 1) for h in heads)
    tails = tuple(tile(h, i, zs[h], jnp.zeros((tq, 1), F32), True) for h in heads)

    def step(jj, carry):
        zs, tails = carry
        j = i - 1 - jj
        nxt = tuple(scores(h, j - 1) for h in heads)
        return nxt, tuple(tile(h, j, zs[h], tails[h], False) for h in heads)

    lax.fori_loop(0, i, step, (nxt, tails))
    out_ref[...] = jnp.concatenate([_head_rms(acc_ref[h], gain_ref[h]) for h in heads],
                                   axis=-1).astype(out_ref.dtype)


def _stick_breaking(sb, gain16, b, s, tq, tk):
    nq = s // tq
    kern = functools.partial(_sb_kernel, tq=tq, tk=tk)
    return pl.pallas_call(
        kern, grid=(b, nq),
        in_specs=[pl.BlockSpec((SB_HEADS, 1, tq, HEAD_DIM), lambda i, j: (0, i, j, 0)),
                  pl.BlockSpec((SB_HEADS, 1, s, HEAD_DIM), lambda i, j: (1, i, 0, 0)),
                  pl.BlockSpec((SB_HEADS, 1, s, HEAD_DIM), lambda i, j: (2, i, 0, 0)),
                  pl.BlockSpec((SB_HEADS, 1, HEAD_DIM), lambda i, j: (3, 0, 0))],
        out_specs=pl.BlockSpec((tq, SB_HEADS * HEAD_DIM), lambda i, j: (i * nq + j, 0)),
        out_shape=jax.ShapeDtypeStruct((b * s, SB_HEADS * HEAD_DIM), BF16),
        scratch_shapes=[pltpu.VMEM((SB_HEADS, tq, HEAD_DIM), F32)],
        compiler_params=_cparams(("parallel", "parallel")),
        name="stick_breaking_attention",
    )(sb, sb, sb, gain16)


def _rms_rows(x, gain):
    return x * lax.rsqrt(jnp.mean(x * x, axis=-1, keepdims=True) + RMS_EPS) * gain


def _residual_kernel(h_ref, on_ref, of_ref, os_ref, p_ref, wo_ref, gf_ref, wg_ref, wu_ref, wd_ref,
                     gp_ref, wp_ref, wpg_ref, gfin_ref, out_ref, *, tf, final):
    n1 = on_ref.shape[1]
    n2 = n1 + of_ref.shape[1]
    mix = (_dot(on_ref[...], wo_ref[:n1, :]) + _dot(of_ref[...], wo_ref[n1:n2, :])
           + _dot(os_ref[...], wo_ref[n2:, :]))
    h = h_ref[...] + mix
    hn = _rms_rows(h, gf_ref[...]).astype(BF16)
    acc = h
    for c in range(wg_ref.shape[1] // tf):
        sl = slice(c * tf, (c + 1) * tf)
        gate = _dot(hn, wg_ref[:, sl])
        up = _dot(hn, wu_ref[:, sl])
        act = (gate * jax.nn.sigmoid(gate) * up).astype(BF16)
        acc = acc + _dot(act, wd_ref[sl, :])
    hn = _rms_rows(acc, gp_ref[...]).astype(BF16)
    gate = jax.nn.sigmoid(_dot(hn, wpg_ref[...]))
    new = acc + _dot(p_ref[...].astype(BF16), wp_ref[...]) * gate
    if final:
        new = _rms_rows(new, gfin_ref[...])
    out_ref[...] = new


def _residual_updates(h2d, o_nsa, o_fox, o_sb, p2d, w_out, g_ffn, wg, wu, wd, g_ple, wp, wpg,
                      g_final, final, tm, tf):
    t, d = h2d.shape
    row = lambda n: pl.BlockSpec((tm, n), lambda i: (i, 0))
    vec = _const_spec((1, d))
    return pl.pallas_call(
        functools.partial(_residual_kernel, tf=tf, final=final), grid=(t // tm,),
        in_specs=[row(d), row(o_nsa.shape[1]), row(o_fox.shape[1]), row(o_sb.shape[1]),
                  row(p2d.shape[1]), _const_spec(w_out.shape), vec, _const_spec(wg.shape),
                  _const_spec(wu.shape), _const_spec(wd.shape), vec, _const_spec(wp.shape),
                  _const_spec(wpg.shape), vec],
        out_specs=row(d),
        out_shape=jax.ShapeDtypeStruct((t, d), F32),
        compiler_params=_cparams(("parallel",)),
        name="residual_updates",
    )(h2d, o_nsa, o_fox, o_sb, p2d, w_out, g_ffn, wg, wu, wd, g_ple, wp, wpg, g_final)


def _overlap_matrix(n_pad, n_blk):
    n_cmp = n_pad - 1
    cmp_start = np.arange(n_pad) * CMP_STRIDE
    blk_start = np.arange(n_blk) * SLC_LEN
    ov = ((cmp_start[:, None] < blk_start[None, :] + SLC_LEN)
          & (cmp_start[:, None] + CMP_LEN > blk_start[None, :])
          & (np.arange(n_pad)[:, None] < n_cmp))
    return jnp.asarray(ov.T.astype(np.float32), dtype=BF16)


def _pick(n, pref):
    return pref if n % pref == 0 else n


def kernel(x, p, positions, norm_mix, w_in, b_nsa_gate, b_forget, nsa_cmp_pos_k, nsa_cmp_w1_k, nsa_cmp_w2_k, nsa_cmp_pos_v, nsa_cmp_w1_v, nsa_cmp_w2_v, head_norm, w_out, norm_ffn, w_ffn_gate, w_ffn_up, w_ffn_down, norm_ple, w_ple_proj, w_ple_gate, norm_final):
    b, s, d = x.shape
    depth = w_in.shape[0]
    t = b * s
    assert s % 512 == 0 and s >= WINDOW + 128
    tm = _pick(t, 512)
    tq_nsa, tk_nsa = 256, 256
    tq_cmp = 512
    tq_fox = 512
    tq_sb, tk_sb = 256, 256
    n_chunk = s // CMP_STRIDE
    n_blk = s // SLC_LEN

    half = HEAD_DIM // 2
    inv_freq = ROPE_THETA ** (-jnp.arange(half, dtype=F32) / half)
    ang = positions.astype(F32)[..., None] * inv_freq
    cos, sin = jnp.cos(ang), jnp.sin(ang)
    cos_t = jnp.concatenate([cos] * 4, axis=-1).reshape(t, LANES)
    sin_t = jnp.concatenate([-sin, sin, -sin, sin], axis=-1).reshape(t, LANES)
    overlap = _overlap_matrix(n_chunk, n_blk)
    blk_onehot = jnp.asarray(
        (np.arange(s)[:, None] // SLC_LEN == np.arange(n_blk)[None, :]).astype(np.float32), dtype=BF16)

    h = x.reshape(t, d)
    for i in range(depth):
        w_perm = _permute_w_in(w_in[i]).astype(BF16)
        bias = _misc_bias(b_nsa_gate[i], b_forget[i])
        qn, kvn, cmp4, fox, sb, misc = _in_projection(
            h, norm_mix[i].reshape(1, d), w_perm, cos_t, sin_t, bias, tm)

        width = CMP_STRIDE * HEAD_DIM
        pos = jnp.stack([nsa_cmp_pos_k[i], nsa_cmp_pos_v[i]]).reshape(2, 2, width)
        w1 = jnp.stack([nsa_cmp_w1_k[i], nsa_cmp_w1_v[i]]).astype(BF16)
        w2 = jnp.stack([nsa_cmp_w2_k[i], nsa_cmp_w2_v[i]]).astype(BF16)
        comp = _compress(cmp4, pos, w1, w2, b, s)

        gain16 = head_norm[i].reshape(N_HEADS, 1, HEAD_DIM)
        qn4 = qn.reshape(NSA_HEADS, b, s, HEAD_DIM)
        oc, selb = _nsa_cmp(qn4, comp, overlap, b, s, tq_cmp)
        kvn4 = kvn.reshape(8, b, s, HEAD_DIM)
        ks_sel = jnp.concatenate(
            [kvn4[:NSA_GROUPS], jnp.broadcast_to(blk_onehot, (NSA_GROUPS, b, s, n_blk))], axis=-1)
        v_t = jnp.swapaxes(kvn4[4:], 2, 3)
        o_nsa = _nsa_main(qn4, ks_sel, kvn4, v_t, selb, oc, misc, gain16, b, s, tq_nsa, tk_nsa)

        log_f = misc[:, 12:16].reshape(b, s // LANES, LANES, FOX_HEADS).transpose(0, 3, 1, 2)
        negc = _neg_cumsum(log_f).reshape(b, FOX_HEADS, s // tq_fox, 1, tq_fox)
        o_fox = _fox(fox.reshape(12, b, s, HEAD_DIM), negc, gain16, b, s, tq_fox)
        o_sb = _stick_breaking(sb.reshape(12, b, s, HEAD_DIM), gain16, b, s, tq_sb, tk_sb)

        h = _residual_updates(
            h, o_nsa, o_fox, o_sb, p[i].reshape(t, -1), w_out[i].astype(BF16),
            norm_ffn[i].reshape(1, d), w_ffn_gate[i].astype(BF16), w_ffn_up[i].astype(BF16),
            w_ffn_down[i].astype(BF16), norm_ple[i].reshape(1, d), w_ple_proj[i].astype(BF16),
            w_ple_gate[i].astype(BF16), norm_final.reshape(1, d), i == depth - 1, tm, 256)
    return h.reshape(b, s, d)
```

```python
import functools

import numpy as np
import jax
import jax.numpy as jnp
from jax import lax
from jax.experimental import pallas as pl
from jax.experimental.pallas import tpu as pltpu

F32 = jnp.float32
BF16 = jnp.bfloat16

HEAD_DIM = 64
NSA_HEADS = 8
NSA_GROUPS = 2
NSA_HPG = 4
FOX_HEADS = 4
SB_HEADS = 4
N_HEADS = 16
CMP_LEN = 32
CMP_STRIDE = 16
CMP_HIDDEN = 128
SLC_LEN = 64
TOP_N = 16
WINDOW = 512
ROPE_THETA = 10000.0
RMS_EPS = 1e-6
NEG_INF = -1e30
FORCE = 1e9
TINY = 1e-20
Q_SCALE = HEAD_DIM ** -0.5

LANES = 128
IN_PAD = 3072
VMEM_LIMIT = 56 * 1024 * 1024

_OFF = dict(nq=0, nkc=512, nvc=640, nks=768, nvs=896, nkw=1024, nvw=1152, ngate=1280,
            fq=1304, fk=1560, fv=1816, ff=2072, sq=2076, sk=2332, sv=2588)


def _dot(a, b):
    return jnp.dot(a, b, preferred_element_type=F32)


def _dot_nt(a, b):
    return lax.dot_general(a, b, (((1,), (1,)), ((), ())), preferred_element_type=F32)


def _cparams(sem):
    return pltpu.CompilerParams(dimension_semantics=sem, vmem_limit_bytes=VMEM_LIMIT)


def _const_spec(shape):
    nd = len(shape)
    return pl.BlockSpec(shape, lambda *_: (0,) * nd, pipeline_mode=pl.Buffered(1))


def _permute_w_in(w):
    d = w.shape[0]

    def cols(name, width):
        return w[:, _OFF[name]:_OFF[name] + width]

    gate = cols("ngate", 24)
    zeros = lambda n: jnp.zeros((d, n), w.dtype)
    parts = [cols("nq", 512), cols("nkc", 128), cols("nks", 128), cols("nkw", 128),
             cols("nvc", 128), cols("nvs", 128), cols("nvw", 128),
             cols("fq", 256), cols("fk", 256), cols("fv", 256),
             cols("sq", 256), cols("sk", 256), cols("sv", 256),
             gate[:, :12], cols("ff", 4), zeros(112), gate[:, 12:], zeros(116)]
    return jnp.concatenate(parts, axis=1)


def _misc_bias(b_gate, b_forget):
    z = lambda n: jnp.zeros((n,), F32)
    return jnp.concatenate([b_gate[:12], b_forget, z(112), b_gate[12:], z(116)]).reshape(1, 256)


def _rope_half(x, cos, sin_signed):
    lane = lax.broadcasted_iota(jnp.int32, x.shape, 1)
    first = (lane % HEAD_DIM) < (HEAD_DIM // 2)
    partner = jnp.where(first, pltpu.roll(x, LANES - 32, 1), pltpu.roll(x, 32, 1))
    return x * cos + partner * sin_signed


def _log_sigmoid(x):
    return jnp.minimum(x, 0.0) - jnp.log(1.0 + jnp.exp(-jnp.abs(x)))


def _inproj_kernel(x_ref, g_ref, w_ref, cos_ref, sin_ref, bias_ref,
                   qn_ref, kvn_ref, cmp_ref, fox_ref, sb_ref, misc_ref):
    x = x_ref[...]
    ms = jnp.mean(x * x, axis=-1, keepdims=True)
    hn = (x * lax.rsqrt(ms + RMS_EPS) * g_ref[...]).astype(BF16)
    cos = cos_ref[...]
    sin = sin_ref[...]

    plan = [(qn_ref, 0, True, True), (qn_ref, 2, True, True),
            (qn_ref, 4, True, True), (qn_ref, 6, True, True),
            (cmp_ref, 0, True, False), (kvn_ref, 0, True, False),
            (kvn_ref, 2, True, False), (cmp_ref, 2, False, False),
            (kvn_ref, 4, False, False), (kvn_ref, 6, False, False),
            (fox_ref, 0, False, True), (fox_ref, 2, False, True),
            (fox_ref, 4, False, False), (fox_ref, 6, False, False),
            (fox_ref, 8, False, False), (fox_ref, 10, False, False),
            (sb_ref, 0, False, True), (sb_ref, 2, False, True),
            (sb_ref, 4, False, False), (sb_ref, 6, False, False),
            (sb_ref, 8, False, False), (sb_ref, 10, False, False)]
    for c in range(IN_PAD // 256):
        res = _dot(hn, w_ref[:, c * 256:(c + 1) * 256])
        if c == IN_PAD // 256 - 1:
            z = res + bias_ref[...]
            lane = lax.broadcasted_iota(jnp.int32, z.shape, 1)
            is_forget = (lane >= 12) & (lane < 16)
            misc_ref[...] = jnp.where(is_forget, _log_sigmoid(z), jax.nn.sigmoid(z))
            continue
        for half in range(2):
            dst, slot, rot, scl = plan[2 * c + half]
            y = res[:, half * LANES:(half + 1) * LANES]
            if rot:
                y = _rope_half(y, cos, sin)
            if scl:
                y = y * Q_SCALE
            y = y.astype(dst.dtype)
            dst[slot] = y[:, :HEAD_DIM]
            dst[slot + 1] = y[:, HEAD_DIM:]


def _in_projection(h2d, gain, w_perm, cos_t, sin_t, bias, tm):
    t, d = h2d.shape
    grid = (t // tm,)
    hm = lambda n, dt: jax.ShapeDtypeStruct((n, t, HEAD_DIM), dt)
    hm_spec = lambda n: pl.BlockSpec((n, tm, HEAD_DIM), lambda i: (0, i, 0))
    return pl.pallas_call(
        _inproj_kernel,
        grid=grid,
        in_specs=[pl.BlockSpec((tm, d), lambda i: (i, 0)),
                  _const_spec((1, d)),
                  _const_spec((d, IN_PAD)),
                  pl.BlockSpec((tm, LANES), lambda i: (i, 0)),
                  pl.BlockSpec((tm, LANES), lambda i: (i, 0)),
                  _const_spec((1, 256))],
        out_specs=[hm_spec(8), hm_spec(8), hm_spec(4), hm_spec(12), hm_spec(12),
                   pl.BlockSpec((tm, 256), lambda i: (i, 0))],
        out_shape=[hm(8, BF16), hm(8, BF16), hm(4, F32), hm(12, BF16), hm(12, BF16),
                   jax.ShapeDtypeStruct((t, 256), F32)],
        compiler_params=_cparams(("parallel",)),
        name="in_projection",
    )(h2d, gain, w_perm, cos_t, sin_t, bias)


def _gelu_tanh(x):
    c = np.float32(np.sqrt(2.0 / np.pi))
    return 0.5 * x * (1.0 + jnp.tanh(c * (x + 0.044715 * (x * x * x))))


def _compress_kernel(ch_ref, pos_ref, w1_ref, w2_ref, out_ref):
    ch = ch_ref[0, 0]
    half = CMP_STRIDE * HEAD_DIM
    top = _dot((ch + pos_ref[0, 0:1, :]).astype(BF16), w1_ref[0, :half, :])
    bot = _dot((ch + pos_ref[0, 1:2, :]).astype(BF16), w1_ref[0, half:, :])
    n = ch.shape[0]
    hidden = top + pltpu.roll(bot, n - 1, 0)
    act = _gelu_tanh(hidden).astype(BF16)
    out_ref[0, 0] = _dot(act, w2_ref[0]).astype(out_ref.dtype)


def _compress(cmp4, pos, w1, w2, b, s):
    n_chunk = s // CMP_STRIDE
    width = CMP_STRIDE * HEAD_DIM
    chunks = cmp4.reshape(4, b, n_chunk, width)
    return pl.pallas_call(
        _compress_kernel,
        grid=(4, b),
        in_specs=[pl.BlockSpec((1, 1, n_chunk, width), lambda w, i: (w, i, 0, 0)),
                  pl.BlockSpec((1, 2, width), lambda w, i: (w // 2, 0, 0)),
                  pl.BlockSpec((1, 2 * width, CMP_HIDDEN), lambda w, i: (w // 2, 0, 0)),
                  pl.BlockSpec((1, CMP_HIDDEN, HEAD_DIM), lambda w, i: (w // 2, 0, 0))],
        out_specs=pl.BlockSpec((1, 1, n_chunk, HEAD_DIM), lambda w, i: (w, i, 0, 0)),
        out_shape=jax.ShapeDtypeStruct((4, b, n_chunk, HEAD_DIM), BF16),
        compiler_params=_cparams(("parallel", "parallel")),
        name="nsa_compress",
    )(chunks, pos, w1, w2)


def _split3(x):
    h1 = x.astype(BF16)
    r1 = x - h1.astype(F32)
    h2 = r1.astype(BF16)
    h3 = (r1 - h2.astype(F32)).astype(BF16)
    return h1, h2, h3


def _cumsum_kernel(lf_ref, out_ref):
    nh, r, _ = lf_ref.shape[1:]
    row = lax.broadcasted_iota(jnp.int32, (LANES, LANES), 0)
    col = lax.broadcasted_iota(jnp.int32, (LANES, LANES), 1)
    upper = jnp.where(row <= col, 1.0, 0.0).astype(BF16)
    rr = lax.broadcasted_iota(jnp.int32, (r, r), 0)
    rc = lax.broadcasted_iota(jnp.int32, (r, r), 1)
    strict = jnp.where(rc < rr, 1.0, 0.0).astype(BF16)
    for h in range(nh):
        x = lf_ref[0, h]
        within = sum(_dot(part, upper) for part in _split3(x))
        tot = jnp.broadcast_to(within[:, LANES - 1:LANES], (r, LANES))
        before = sum(_dot(strict, part) for part in _split3(tot))
        out_ref[0, h] = -(within + before)


def _neg_cumsum(lf):
    b, nh, r, _ = lf.shape
    spec = pl.BlockSpec((1, nh, r, LANES), lambda i: (i, 0, 0, 0))
    return pl.pallas_call(
        _cumsum_kernel, grid=(b,), in_specs=[spec], out_specs=spec,
        out_shape=jax.ShapeDtypeStruct(lf.shape, F32),
        compiler_params=_cparams(("parallel",)),
        name="forget_cumsum",
    )(lf)


def _nsa_cmp_kernel(q_ref, kc_ref, vc_ref, ovt_ref, oc_ref, selb_ref, *, tq, n_sel):
    q0 = pl.program_id(2) * tq
    q4 = q_ref[:, 0].reshape(NSA_HPG * tq, HEAD_DIM)
    kc = kc_ref[0, 0]
    vc = vc_ref[0, 0]
    n_pad = kc.shape[0]
    n_blk = ovt_ref.shape[0]
    s = _dot_nt(q4, kc).reshape(NSA_HPG, tq, n_pad)
    t = q0 + lax.broadcasted_iota(jnp.int32, (1, tq, 1), 1)
    n = lax.broadcasted_iota(jnp.int32, (1, 1, n_pad), 2)
    mask = (n * CMP_STRIDE + (CMP_LEN - 1) <= t) & (n < n_pad - 1)
    l = jnp.where(mask, s, NEG_INF)
    m = jnp.max(l, axis=-1, keepdims=True)
    e = jnp.where(mask, jnp.exp(l - m), 0.0)
    p = e / jnp.maximum(jnp.sum(e, axis=-1, keepdims=True), TINY)
    pb = p.astype(BF16).reshape(NSA_HPG * tq, n_pad)
    oc_ref[:, 0] = _dot(pb, vc).reshape(NSA_HPG, tq, HEAD_DIM)
    imp4 = _dot_nt(ovt_ref[...], pb)
    imp = sum(imp4[:, hh * tq:(hh + 1) * tq] for hh in range(NSA_HPG))
    t2 = q0 + lax.broadcasted_iota(jnp.int32, (1, tq), 1)
    blk = lax.broadcasted_iota(jnp.int32, (n_blk, tq), 0)
    cur = t2 // SLC_LEN
    forced = (blk == 0) | (blk == cur) | (blk == cur - 1)
    score = jnp.where(forced, FORCE, jnp.where(blk <= cur, imp, -FORCE))

    sub = lax.broadcasted_iota(jnp.int32, (8, tq), 0)
    groups = [score[8 * v:8 * v + 8] for v in range(n_blk // 8)]
    rank = [jnp.zeros((8, tq), F32) for _ in groups]
    for i in range(n_blk):
        row = jnp.broadcast_to(score[i:i + 1], (8, tq))
        for v, grp in enumerate(groups):
            if v > i // 8:
                beats = jnp.where(row >= grp, 1.0, 0.0)
            elif v < i // 8:
                beats = jnp.where(row > grp, 1.0, 0.0)
            else:
                beats = jnp.where(sub > i % 8, jnp.where(row >= grp, 1.0, 0.0),
                                  jnp.where(row > grp, 1.0, 0.0))
            rank[v] = rank[v] + beats
    sel_bias = jnp.where(jnp.concatenate(rank, axis=0) < float(n_sel), 0.0, NEG_INF)
    selb_ref[0, 0] = sel_bias.T.astype(selb_ref.dtype)


def _nsa_cmp(qn, comp, overlap, b, s, tq):
    n_pad = comp.shape[2]
    n_blk = overlap.shape[0]
    grid = (b, NSA_GROUPS, s // tq)
    kern = functools.partial(_nsa_cmp_kernel, tq=tq, n_sel=min(TOP_N, n_blk))
    return pl.pallas_call(
        kern, grid=grid,
        in_specs=[pl.BlockSpec((NSA_HPG, 1, tq, HEAD_DIM), lambda i, g, j: (g, i, j, 0)),
                  pl.BlockSpec((1, 1, n_pad, HEAD_DIM), lambda i, g, j: (g, i, 0, 0)),
                  pl.BlockSpec((1, 1, n_pad, HEAD_DIM), lambda i, g, j: (2 + g, i, 0, 0)),
                  _const_spec((n_blk, n_pad))],
        out_specs=[pl.BlockSpec((NSA_HPG, 1, tq, HEAD_DIM), lambda i, g, j: (g, i, j, 0)),
                   pl.BlockSpec((1, 1, tq, n_blk), lambda i, g, j: (i, g, j, 0))],
        out_shape=[jax.ShapeDtypeStruct((NSA_HEADS, b, s, HEAD_DIM), F32),
                   jax.ShapeDtypeStruct((b, NSA_GROUPS, s, n_blk), BF16)],
        compiler_params=_cparams(("parallel", "parallel", "parallel")),
        name="nsa_compressed_select",
    )(qn, comp, comp, overlap)


def _head_rms(o, gain):
    return o * lax.rsqrt(jnp.mean(o * o, axis=-1, keepdims=True) + RMS_EPS) * gain


def _nsa_main_kernel(q_ref, ks_ref, kw_ref, vst_ref, vwt_ref, selb_ref, oc_ref, gate_ref, gain_ref,
                     out_ref, *, tq, tk):
    q0 = pl.program_id(2) * tq
    hq = NSA_HPG * tq
    q4 = q_ref[:, 0].reshape(hq, HEAD_DIM)
    selb = selb_ref[0, 0]
    q_sel = jnp.concatenate(
        [jnp.concatenate([q_ref[hh, 0], selb], axis=-1) for hh in range(NSA_HPG)], axis=0)
    t = q0 + lax.broadcasted_iota(jnp.int32, (1, tq), 1)

    def per_head(bias):
        return jnp.concatenate([bias] * NSA_HPG, axis=1)

    def slc_scores(j):
        start = pl.multiple_of(j * tk, tk)
        return _dot_nt(ks_ref[0, 0, pl.ds(start, tk), :], q_sel)

    def slc_tile(j, sc, carry, causal):
        m, l, acc = carry
        start = pl.multiple_of(j * tk, tk)
        if causal:
            kpos = start + lax.broadcasted_iota(jnp.int32, (tk, 1), 0)
            sc = sc + per_head(jnp.where(kpos <= t, 0.0, NEG_INF))
        m_new = jnp.maximum(m, jnp.max(sc, axis=0, keepdims=True))
        alpha = jnp.exp(m - m_new)
        p = jnp.exp(sc - m_new)
        l = alpha * l + jnp.sum(p, axis=0, keepdims=True)
        pv = _dot(vst_ref[0, 0, :, pl.ds(start, tk)], p.astype(BF16))
        return m_new, l, alpha * acc + pv

    init = (jnp.full((1, hq), NEG_INF, F32), jnp.zeros((1, hq), F32),
            jnp.zeros((HEAD_DIM, hq), F32))
    n_full = q0 // tk
    def slc_step(j, carry):
        nxt = slc_scores(j + 1)
        return (nxt,) + slc_tile(j, carry[0], carry[1:], False)

    carry = lax.fori_loop(0, n_full, slc_step, (slc_scores(0),) + init)
    _, l_s, acc_s = slc_tile(n_full, carry[0], carry[1:], True)
    o_s = (acc_s / l_s).T

    span = WINDOW + tq
    w0 = pl.multiple_of(jnp.maximum(q0 - WINDOW, 0), tq)
    kpos = w0 + lax.broadcasted_iota(jnp.int32, (span, 1), 0)
    wbias = jnp.where((kpos <= t) & (t - kpos < WINDOW), 0.0, NEG_INF)
    sw = _dot_nt(kw_ref[0, 0, pl.ds(w0, span), :], q4) + per_head(wbias)
    mw = jnp.max(sw, axis=0, keepdims=True)
    ew = jnp.exp(sw - mw)
    lw = jnp.sum(ew, axis=0, keepdims=True)
    o_w = (_dot(vwt_ref[0, 0, :, pl.ds(w0, span)], ew.astype(BF16)) / lw).T

    gates = gate_ref[...]
    heads = []
    for hh in range(NSA_HPG):
        rows = slice(hh * tq, (hh + 1) * tq)
        g0 = gates[:, 3 * hh:3 * hh + 1]
        g1 = gates[:, 3 * hh + 1:3 * hh + 2]
        g2 = gates[:, 3 * hh + 2:3 * hh + 3]
        o = g0 * oc_ref[hh, 0] + g1 * o_s[rows] + g2 * o_w[rows]
        heads.append(_head_rms(o, gain_ref[hh]).astype(out_ref.dtype))
    out_ref[...] = jnp.concatenate(heads, axis=-1)


def _nsa_main(qn, ks_sel, kvn, v_t, selb, oc, misc, gain16, b, s, tq, tk):
    nq = s // tq
    n_blk = selb.shape[-1]
    grid = (b, NSA_GROUPS, nq)
    vt_spec = lambda base: pl.BlockSpec((1, 1, HEAD_DIM, s), lambda i, g, j: (base + g, i, 0, 0))
    kern = functools.partial(_nsa_main_kernel, tq=tq, tk=tk)
    return pl.pallas_call(
        kern, grid=grid,
        in_specs=[pl.BlockSpec((NSA_HPG, 1, tq, HEAD_DIM), lambda i, g, j: (g, i, j, 0)),
                  pl.BlockSpec((1, 1, s, HEAD_DIM + n_blk), lambda i, g, j: (g, i, 0, 0)),
                  pl.BlockSpec((1, 1, s, HEAD_DIM), lambda i, g, j: (2 + g, i, 0, 0)),
                  vt_spec(0), vt_spec(2),
                  pl.BlockSpec((1, 1, tq, n_blk), lambda i, g, j: (i, g, j, 0)),
                  pl.BlockSpec((NSA_HPG, 1, tq, HEAD_DIM), lambda i, g, j: (g, i, j, 0)),
                  pl.BlockSpec((tq, LANES), lambda i, g, j: (i * nq + j, g)),
                  pl.BlockSpec((NSA_HPG, 1, HEAD_DIM), lambda i, g, j: (g, 0, 0))],
        out_specs=pl.BlockSpec((tq, NSA_HPG * HEAD_DIM), lambda i, g, j: (i * nq + j, g)),
        out_shape=jax.ShapeDtypeStruct((b * s, NSA_HEADS * HEAD_DIM), BF16),
        compiler_params=_cparams(("parallel", "parallel", "parallel")),
        name="nsa_selected_window",
    )(qn, ks_sel, kvn, v_t, v_t, selb, oc, misc, gain16)


def _fox_kernel(q_ref, k_ref, v_ref, nc_ref, gain_ref, out_ref, o_scr, *, tq):
    i = pl.program_id(1)
    q0 = i * tq
    tk = tq
    row = q0 + lax.broadcasted_iota(jnp.int32, (tq, 1), 0)

    def head_pair(hp, _):
        hs = (2 * hp, 2 * hp + 1)
        qs = tuple(q_ref[h, 0] for h in hs)

        def tile(h, q, j, carry, causal):
            m, l, acc = carry
            start = pl.multiple_of(j * tk, tk)
            k = k_ref[h, 0, pl.ds(start, tk), :]
            v = v_ref[h, 0, pl.ds(start, tk), :]
            sc = _dot_nt(q, k) + nc_ref[0, h, j]
            if causal:
                kpos = start + lax.broadcasted_iota(jnp.int32, (1, tk), 1)
                sc = jnp.where(kpos <= row, sc, NEG_INF)
            m_new = jnp.maximum(m, jnp.max(sc, axis=-1, keepdims=True))
            alpha = jnp.exp(m - m_new)
            p = jnp.exp(sc - m_new)
            l = alpha * l + jnp.sum(p, axis=-1, keepdims=True)
            return m_new, l, alpha * acc + _dot(p.astype(BF16), v)

        def both(j, carries, causal):
            return tuple(tile(h, q, j, c, causal) for h, q, c in zip(hs, qs, carries))

        init = (jnp.full((tq, 1), NEG_INF, F32), jnp.zeros((tq, 1), F32),
                jnp.zeros((tq, HEAD_DIM), F32))
        carries = lax.fori_loop(0, i, lambda j, c: both(j, c, False), (init, init))
        for h, (_, l, acc) in zip(hs, both(i, carries, True)):
            o_scr[h] = _head_rms(acc / l, gain_ref[h])
        return 0

    lax.fori_loop(0, FOX_HEADS // 2, head_pair, 0)
    out_ref[...] = jnp.concatenate([o_scr[h] for h in range(FOX_HEADS)],
                                   axis=-1).astype(out_ref.dtype)


def _fox(fox, negc, gain16, b, s, tq):
    nq = s // tq
    kern = functools.partial(_fox_kernel, tq=tq)
    return pl.pallas_call(
        kern, grid=(b, nq),
        in_specs=[pl.BlockSpec((FOX_HEADS, 1, tq, HEAD_DIM), lambda i, j: (0, i, j, 0)),
                  pl.BlockSpec((FOX_HEADS, 1, s, HEAD_DIM), lambda i, j: (1, i, 0, 0)),
                  pl.BlockSpec((FOX_HEADS, 1, s, HEAD_DIM), lambda i, j: (2, i, 0, 0)),
                  pl.BlockSpec((1, FOX_HEADS, nq, 1, tq), lambda i, j: (i, 0, 0, 0, 0)),
                  pl.BlockSpec((FOX_HEADS, 1, HEAD_DIM), lambda i, j: (2, 0, 0))],
        out_specs=pl.BlockSpec((tq, FOX_HEADS * HEAD_DIM), lambda i, j: (i * nq + j, 0)),
        out_shape=jax.ShapeDtypeStruct((b * s, FOX_HEADS * HEAD_DIM), BF16),
        scratch_shapes=[pltpu.VMEM((FOX_HEADS, tq, HEAD_DIM), F32)],
        compiler_params=_cparams(("parallel", "parallel")),
        name="fox_attention",
    )(fox, fox, fox, negc, gain16)


def _sb_kernel(q_ref, k_ref, v_ref, gain_ref, out_ref, acc_ref, *, tq, tk):
    i = pl.program_id(1)
    q0 = i * tq
    row = q0 + lax.broadcasted_iota(jnp.int32, (tq, 1), 0)
    r_i = lax.broadcasted_iota(jnp.int32, (tk, tk), 0)
    c_i = lax.broadcasted_iota(jnp.int32, (tk, tk), 1)
    suffix = jnp.where(r_i >= c_i, 1.0, 0.0).astype(BF16)
    assert tq == tk
    heads = range(SB_HEADS)
    qs = [q_ref[h, 0] for h in heads]
    acc_ref[...] = jnp.zeros_like(acc_ref)

    def scores(h, j):
        start = pl.multiple_of(jnp.maximum(j, 0) * tk, tk)
        return _dot_nt(qs[h], k_ref[h, 0, pl.ds(start, tk), :])

    def tile(h, j, z, tail, masked):
        start = pl.multiple_of(j * tk, tk)
        v = v_ref[h, 0, pl.ds(start, tk), :]
        lr = -(jnp.maximum(z, 0.0) + jnp.log(1.0 + jnp.exp(-jnp.abs(z))))
        if masked:
            mask = (start + lax.broadcasted_iota(jnp.int32, (1, tk), 1)) < row
            lr = jnp.where(mask, lr, 0.0)
        hi = lr.astype(BF16)
        lo = (lr - hi.astype(F32)).astype(BF16)
        both = _dot(jnp.concatenate([hi, lo], axis=0), suffix)
        cs = both[:tq] + both[tq:]
        a = jnp.exp(z + cs + tail)
        if masked:
            a = jnp.where(mask, a, 0.0)
        acc_ref[h] += _dot(a.astype(BF16), v)
        return tail + cs[:, 0:1]

    zs = tuple(scores(h, i) for h in heads)
    nxt = tuple(scores(h, i - 1) for h in heads)
    tails = tuple(tile(h, i, zs[h], jnp.zeros((tq, 1), F32), True) for h in heads)

    def step(jj, carry):
        zs, tails = carry
        j = i - 1 - jj
        nxt = tuple(scores(h, j - 1) for h in heads)
        return nxt, tuple(tile(h, j, zs[h], tails[h], False) for h in heads)

    lax.fori_loop(0, i, step, (nxt, tails))
    out_ref[...] = jnp.concatenate([_head_rms(acc_ref[h], gain_ref[h]) for h in heads],
                                   axis=-1).astype(out_ref.dtype)


def _stick_breaking(sb, gain16, b, s, tq, tk):
    nq = s // tq
    kern = functools.partial(_sb_kernel, tq=tq, tk=tk)
    return pl.pallas_call(
        kern, grid=(b, nq),
        in_specs=[pl.BlockSpec((SB_HEADS, 1, tq, HEAD_DIM), lambda i, j: (0, i, j, 0)),
                  pl.BlockSpec((SB_HEADS, 1, s, HEAD_DIM), lambda i, j: (1, i, 0, 0)),
                  pl.BlockSpec((SB_HEADS, 1, s, HEAD_DIM), lambda i, j: (2, i, 0, 0)),
                  pl.BlockSpec((SB_HEADS, 1, HEAD_DIM), lambda i, j: (3, 0, 0))],
        out_specs=pl.BlockSpec((tq, SB_HEADS * HEAD_DIM), lambda i, j: (i * nq + j, 0)),
        out_shape=jax.ShapeDtypeStruct((b * s, SB_HEADS * HEAD_DIM), BF16),
        scratch_shapes=[pltpu.VMEM((SB_HEADS, tq, HEAD_DIM), F32)],
        compiler_params=_cparams(("parallel", "parallel")),
        name="stick_breaking_attention",
    )(sb, sb, sb, gain16)


def _rms_rows(x, gain):
    return x * lax.rsqrt(jnp.mean(x * x, axis=-1, keepdims=True) + RMS_EPS) * gain


def _residual_kernel(h_ref, on_ref, of_ref, os_ref, p_ref, wo_ref, gf_ref, wg_ref, wu_ref, wd_ref,
                     gp_ref, wp_ref, wpg_ref, gfin_ref, out_ref, *, tf, final):
    n1 = on_ref.shape[1]
    n2 = n1 + of_ref.shape[1]
    mix = (_dot(on_ref[...], wo_ref[:n1, :]) + _dot(of_ref[...], wo_ref[n1:n2, :])
           + _dot(os_ref[...], wo_ref[n2:, :]))
    h = h_ref[...] + mix
    hn = _rms_rows(h, gf_ref[...]).astype(BF16)
    acc = h
    for c in range(wg_ref.shape[1] // tf):
        sl = slice(c * tf, (c + 1) * tf)
        gate = _dot(hn, wg_ref[:, sl])
        up = _dot(hn, wu_ref[:, sl])
        act = (gate * jax.nn.sigmoid(gate) * up).astype(BF16)
        acc = acc + _dot(act, wd_ref[sl, :])
    hn = _rms_rows(acc, gp_ref[...]).astype(BF16)
    gate = jax.nn.sigmoid(_dot(hn, wpg_ref[...]))
    new = acc + _dot(p_ref[...].astype(BF16), wp_ref[...]) * gate
    if final:
        new = _rms_rows(new, gfin_ref[...])
    out_ref[...] = new


def _residual_updates(h2d, o_nsa, o_fox, o_sb, p2d, w_out, g_ffn, wg, wu, wd, g_ple, wp, wpg,
                      g_final, final, tm, tf):
    t, d = h2d.shape
    row = lambda n: pl.BlockSpec((tm, n), lambda i: (i, 0))
    vec = _const_spec((1, d))
    return pl.pallas_call(
        functools.partial(_residual_kernel, tf=tf, final=final), grid=(t // tm,),
        in_specs=[row(d), row(o_nsa.shape[1]), row(o_fox.shape[1]), row(o_sb.shape[1]),
                  row(p2d.shape[1]), _const_spec(w_out.shape), vec, _const_spec(wg.shape),
                  _const_spec(wu.shape), _const_spec(wd.shape), vec, _const_spec(wp.shape),
                  _const_spec(wpg.shape), vec],
        out_specs=row(d),
        out_shape=jax.ShapeDtypeStruct((t, d), F32),
        compiler_params=_cparams(("parallel",)),
        name="residual_updates",
    )(h2d, o_nsa, o_fox, o_sb, p2d, w_out, g_ffn, wg, wu, wd, g_ple, wp, wpg, g_final)


def _overlap_matrix(n_pad, n_blk):
    n_cmp = n_pad - 1
    cmp_start = np.arange(n_pad) * CMP_STRIDE
    blk_start = np.arange(n_blk) * SLC_LEN
    ov = ((cmp_start[:, None] < blk_start[None, :] + SLC_LEN)
          & (cmp_start[:, None] + CMP_LEN > blk_start[None, :])
          & (np.arange(n_pad)[:, None] < n_cmp))
    return jnp.asarray(ov.T.astype(np.float32), dtype=BF16)


def _pick(n, pref):
    return pref if n % pref == 0 else n


def kernel(x, p, positions, norm_mix, w_in, b_nsa_gate, b_forget, nsa_cmp_pos_k, nsa_cmp_w1_k, nsa_cmp_w2_k, nsa_cmp_pos_v, nsa_cmp_w1_v, nsa_cmp_w2_v, head_norm, w_out, norm_ffn, w_ffn_gate, w_ffn_up, w_ffn_down, norm_ple, w_ple_proj, w_ple_gate, norm_final):
    b, s, d = x.shape
    depth = w_in.shape[0]
    t = b * s
    assert s % 512 == 0 and s >= WINDOW + 128
    tm = _pick(t, 512)
    tq_nsa, tk_nsa = 256, 256
    tq_cmp = 1024
    tq_fox = 512
    tq_sb, tk_sb = 256, 256
    n_chunk = s // CMP_STRIDE
    n_blk = s // SLC_LEN

    half = HEAD_DIM // 2
    inv_freq = ROPE_THETA ** (-jnp.arange(half, dtype=F32) / half)
    ang = positions.astype(F32)[..., None] * inv_freq
    cos, sin = jnp.cos(ang), jnp.sin(ang)
    cos_t = jnp.concatenate([cos] * 4, axis=-1).reshape(t, LANES)
    sin_t = jnp.concatenate([-sin, sin, -sin, sin], axis=-1).reshape(t, LANES)
    overlap = _overlap_matrix(n_chunk, n_blk)
    blk_onehot = jnp.asarray(
        (np.arange(s)[:, None] // SLC_LEN == np.arange(n_blk)[None, :]).astype(np.float32), dtype=BF16)

    h = x.reshape(t, d)
    for i in range(depth):
        w_perm = _permute_w_in(w_in[i]).astype(BF16)
        bias = _misc_bias(b_nsa_gate[i], b_forget[i])
        qn, kvn, cmp4, fox, sb, misc = _in_projection(
            h, norm_mix[i].reshape(1, d), w_perm, cos_t, sin_t, bias, tm)

        width = CMP_STRIDE * HEAD_DIM
        pos = jnp.stack([nsa_cmp_pos_k[i], nsa_cmp_pos_v[i]]).reshape(2, 2, width)
        w1 = jnp.stack([nsa_cmp_w1_k[i], nsa_cmp_w1_v[i]]).astype(BF16)
        w2 = jnp.stack([nsa_cmp_w2_k[i], nsa_cmp_w2_v[i]]).astype(BF16)
        comp = _compress(cmp4, pos, w1, w2, b, s)

        gain16 = head_norm[i].reshape(N_HEADS, 1, HEAD_DIM)
        qn4 = qn.reshape(NSA_HEADS, b, s, HEAD_DIM)
        oc, selb = _nsa_cmp(qn4, comp, overlap, b, s, tq_cmp)
        kvn4 = kvn.reshape(8, b, s, HEAD_DIM)
        ks_sel = jnp.concatenate(
            [kvn4[:NSA_GROUPS], jnp.broadcast_to(blk_onehot, (NSA_GROUPS, b, s, n_blk))], axis=-1)
        v_t = jnp.swapaxes(kvn4[4:], 2, 3)
        o_nsa = _nsa_main(qn4, ks_sel, kvn4, v_t, selb, oc, misc, gain16, b, s, tq_nsa, tk_nsa)

        log_f = misc[:, 12:16].reshape(b, s // LANES, LANES, FOX_HEADS).transpose(0, 3, 1, 2)
        negc = _neg_cumsum(log_f).reshape(b, FOX_HEADS, s // tq_fox, 1, tq_fox)
        o_fox = _fox(fox.reshape(12, b, s, HEAD_DIM), negc, gain16, b, s, tq_fox)
        o_sb = _stick_breaking(sb.reshape(12, b, s, HEAD_DIM), gain16, b, s, tq_sb, tk_sb)

        h = _residual_updates(
            h, o_nsa, o_fox, o_sb, p[i].reshape(t, -1), w_out[i].astype(BF16),
            norm_ffn[i].reshape(1, d), w_ffn_gate[i].astype(BF16), w_ffn_up[i].astype(BF16),
            w_ffn_down[i].astype(BF16), norm_ple[i].reshape(1, d), w_ple_proj[i].astype(BF16),
            w_ple_gate[i].astype(BF16), norm_final.reshape(1, d), i == depth - 1, tm, 256)
    return h.reshape(b, s, d)
```

```python
import functools

import numpy as np
import jax
import jax.numpy as jnp
from jax import lax
from jax.experimental import pallas as pl
from jax.experimental.pallas import tpu as pltpu

F32 = jnp.float32
BF16 = jnp.bfloat16

HEAD_DIM = 64
NSA_HEADS = 8
NSA_GROUPS = 2
NSA_HPG = 4
FOX_HEADS = 4
SB_HEADS = 4
N_HEADS = 16
CMP_LEN = 32
CMP_STRIDE = 16
CMP_HIDDEN = 128
SLC_LEN = 64
TOP_N = 16
WINDOW = 512
ROPE_THETA = 10000.0
RMS_EPS = 1e-6
NEG_INF = -1e30
FORCE = 1e9
TINY = 1e-20
Q_SCALE = HEAD_DIM ** -0.5

LANES = 128
IN_PAD = 3072
VMEM_LIMIT = 56 * 1024 * 1024

_OFF = dict(nq=0, nkc=512, nvc=640, nks=768, nvs=896, nkw=1024, nvw=1152, ngate=1280,
            fq=1304, fk=1560, fv=1816, ff=2072, sq=2076, sk=2332, sv=2588)


def _dot(a, b):
    return jnp.dot(a, b, preferred_element_type=F32)


def _dot_nt(a, b):
    return lax.dot_general(a, b, (((1,), (1,)), ((), ())), preferred_element_type=F32)


def _cparams(sem):
    return pltpu.CompilerParams(dimension_semantics=sem, vmem_limit_bytes=VMEM_LIMIT)


def _const_spec(shape):
    nd = len(shape)
    return pl.BlockSpec(shape, lambda *_: (0,) * nd, pipeline_mode=pl.Buffered(1))


def _permute_w_in(w):
    d = w.shape[0]

    def cols(name, width):
        return w[:, _OFF[name]:_OFF[name] + width]

    gate = cols("ngate", 24)
    zeros = lambda n: jnp.zeros((d, n), w.dtype)
    parts = [cols("nq", 512), cols("nkc", 128), cols("nks", 128), cols("nkw", 128),
             cols("nvc", 128), cols("nvs", 128), cols("nvw", 128),
             cols("fq", 256), cols("fk", 256), cols("fv", 256),
             cols("sq", 256), cols("sk", 256), cols("sv", 256),
             gate[:, :12], cols("ff", 4), zeros(112), gate[:, 12:], zeros(116)]
    return jnp.concatenate(parts, axis=1)


def _misc_bias(b_gate, b_forget):
    z = lambda n: jnp.zeros((n,), F32)
    return jnp.concatenate([b_gate[:12], b_forget, z(112), b_gate[12:], z(116)]).reshape(1, 256)


def _rope_half(x, cos, sin_signed):
    lane = lax.broadcasted_iota(jnp.int32, x.shape, 1)
    first = (lane % HEAD_DIM) < (HEAD_DIM // 2)
    partner = jnp.where(first, pltpu.roll(x, LANES - 32, 1), pltpu.roll(x, 32, 1))
    return x * cos + partner * sin_signed


def _log_sigmoid(x):
    return jnp.minimum(x, 0.0) - jnp.log(1.0 + jnp.exp(-jnp.abs(x)))


def _inproj_kernel(x_ref, g_ref, w_ref, cos_ref, sin_ref, bias_ref,
                   qn_ref, kvn_ref, cmp_ref, fox_ref, sb_ref, misc_ref):
    x = x_ref[...]
    ms = jnp.mean(x * x, axis=-1, keepdims=True)
    hn = (x * lax.rsqrt(ms + RMS_EPS) * g_ref[...]).astype(BF16)
    cos = cos_ref[...]
    sin = sin_ref[...]

    plan = [(qn_ref, 0, True, True), (qn_ref, 2, True, True),
            (qn_ref, 4, True, True), (qn_ref, 6, True, True),
            (cmp_ref, 0, True, False), (kvn_ref, 0, True, False),
            (kvn_ref, 2, True, False), (cmp_ref, 2, False, False),
            (kvn_ref, 4, False, False), (kvn_ref, 6, False, False),
            (fox_ref, 0, False, True), (fox_ref, 2, False, True),
            (fox_ref, 4, False, False), (fox_ref, 6, False, False),
            (fox_ref, 8, False, False), (fox_ref, 10, False, False),
            (sb_ref, 0, False, True), (sb_ref, 2, False, True),
            (sb_ref, 4, False, False), (sb_ref, 6, False, False),
            (sb_ref, 8, False, False), (sb_ref, 10, False, False)]
    for c in range(IN_PAD // 256):
        res = _dot(hn, w_ref[:, c * 256:(c + 1) * 256])
        if c == IN_PAD // 256 - 1:
            z = res + bias_ref[...]
            lane = lax.broadcasted_iota(jnp.int32, z.shape, 1)
            is_forget = (lane >= 12) & (lane < 16)
            misc_ref[...] = jnp.where(is_forget, _log_sigmoid(z), jax.nn.sigmoid(z))
            continue
        for half in range(2):
            dst, slot, rot, scl = plan[2 * c + half]
            y = res[:, half * LANES:(half + 1) * LANES]
            if rot:
                y = _rope_half(y, cos, sin)
            if scl:
                y = y * Q_SCALE
            y = y.astype(dst.dtype)
            dst[slot] = y[:, :HEAD_DIM]
            dst[slot + 1] = y[:, HEAD_DIM:]


def _in_projection(h2d, gain, w_perm, cos_t, sin_t, bias, tm):
    t, d = h2d.shape
    grid = (t // tm,)
    hm = lambda n, dt: jax.ShapeDtypeStruct((n, t, HEAD_DIM), dt)
    hm_spec = lambda n: pl.BlockSpec((n, tm, HEAD_DIM), lambda i: (0, i, 0))
    return pl.pallas_call(
        _inproj_kernel,
        grid=grid,
        in_specs=[pl.BlockSpec((tm, d), lambda i: (i, 0)),
                  _const_spec((1, d)),
                  _const_spec((d, IN_PAD)),
                  pl.BlockSpec((tm, LANES), lambda i: (i, 0)),
                  pl.BlockSpec((tm, LANES), lambda i: (i, 0)),
                  _const_spec((1, 256))],
        out_specs=[hm_spec(8), hm_spec(8), hm_spec(4), hm_spec(12), hm_spec(12),
                   pl.BlockSpec((tm, 256), lambda i: (i, 0))],
        out_shape=[hm(8, BF16), hm(8, BF16), hm(4, F32), hm(12, BF16), hm(12, BF16),
                   jax.ShapeDtypeStruct((t, 256), F32)],
        compiler_params=_cparams(("parallel",)),
        name="in_projection",
    )(h2d, gain, w_perm, cos_t, sin_t, bias)


def _gelu_tanh(x):
    c = np.float32(np.sqrt(2.0 / np.pi))
    return 0.5 * x * (1.0 + jnp.tanh(c * (x + 0.044715 * (x * x * x))))


def _compress_kernel(ch_ref, pos_ref, w1_ref, w2_ref, out_ref):
    ch = ch_ref[0, 0]
    half = CMP_STRIDE * HEAD_DIM
    top = _dot((ch + pos_ref[0, 0:1, :]).astype(BF16), w1_ref[0, :half, :])
    bot = _dot((ch + pos_ref[0, 1:2, :]).astype(BF16), w1_ref[0, half:, :])
    n = ch.shape[0]
    hidden = top + pltpu.roll(bot, n - 1, 0)
    act = _gelu_tanh(hidden).astype(BF16)
    out_ref[0, 0] = _dot(act, w2_ref[0]).astype(out_ref.dtype)


def _compress(cmp4, pos, w1, w2, b, s):
    n_chunk = s // CMP_STRIDE
    width = CMP_STRIDE * HEAD_DIM
    chunks = cmp4.reshape(4, b, n_chunk, width)
    return pl.pallas_call(
        _compress_kernel,
        grid=(4, b),
        in_specs=[pl.BlockSpec((1, 1, n_chunk, width), lambda w, i: (w, i, 0, 0)),
                  pl.BlockSpec((1, 2, width), lambda w, i: (w // 2, 0, 0)),
                  pl.BlockSpec((1, 2 * width, CMP_HIDDEN), lambda w, i: (w // 2, 0, 0)),
                  pl.BlockSpec((1, CMP_HIDDEN, HEAD_DIM), lambda w, i: (w // 2, 0, 0))],
        out_specs=pl.BlockSpec((1, 1, n_chunk, HEAD_DIM), lambda w, i: (w, i, 0, 0)),
        out_shape=jax.ShapeDtypeStruct((4, b, n_chunk, HEAD_DIM), BF16),
        compiler_params=_cparams(("parallel", "parallel")),
        name="nsa_compress",
    )(chunks, pos, w1, w2)


def _split3(x):
    h1 = x.astype(BF16)
    r1 = x - h1.astype(F32)
    h2 = r1.astype(BF16)
    h3 = (r1 - h2.astype(F32)).astype(BF16)
    return h1, h2, h3


def _cumsum_kernel(lf_ref, out_ref):
    nh, r, _ = lf_ref.shape[1:]
    row = lax.broadcasted_iota(jnp.int32, (LANES, LANES), 0)
    col = lax.broadcasted_iota(jnp.int32, (LANES, LANES), 1)
    upper = jnp.where(row <= col, 1.0, 0.0).astype(BF16)
    rr = lax.broadcasted_iota(jnp.int32, (r, r), 0)
    rc = lax.broadcasted_iota(jnp.int32, (r, r), 1)
    strict = jnp.where(rc < rr, 1.0, 0.0).astype(BF16)
    for h in range(nh):
        x = lf_ref[0, h]
        within = sum(_dot(part, upper) for part in _split3(x))
        tot = jnp.broadcast_to(within[:, LANES - 1:LANES], (r, LANES))
        before = sum(_dot(strict, part) for part in _split3(tot))
        out_ref[0, h] = -(within + before)


def _neg_cumsum(lf):
    b, nh, r, _ = lf.shape
    spec = pl.BlockSpec((1, nh, r, LANES), lambda i: (i, 0, 0, 0))
    return pl.pallas_call(
        _cumsum_kernel, grid=(b,), in_specs=[spec], out_specs=spec,
        out_shape=jax.ShapeDtypeStruct(lf.shape, F32),
        compiler_params=_cparams(("parallel",)),
        name="forget_cumsum",
    )(lf)


def _nsa_cmp_kernel(q_ref, kc_ref, vc_ref, ovt_ref, oc_ref, selb_ref, *, tq, n_sel):
    q0 = pl.program_id(2) * tq
    q4 = q_ref[:, 0].reshape(NSA_HPG * tq, HEAD_DIM)
    kc = kc_ref[0, 0]
    vc = vc_ref[0, 0]
    n_pad = kc.shape[0]
    n_blk = ovt_ref.shape[0]
    s = _dot_nt(q4, kc).reshape(NSA_HPG, tq, n_pad)
    t = q0 + lax.broadcasted_iota(jnp.int32, (1, tq, 1), 1)
    n = lax.broadcasted_iota(jnp.int32, (1, 1, n_pad), 2)
    mask = (n * CMP_STRIDE + (CMP_LEN - 1) <= t) & (n < n_pad - 1)
    l = jnp.where(mask, s, NEG_INF)
    m = jnp.max(l, axis=-1, keepdims=True)
    e = jnp.where(mask, jnp.exp(l - m), 0.0)
    p = e / jnp.maximum(jnp.sum(e, axis=-1, keepdims=True), TINY)
    pb = p.astype(BF16).reshape(NSA_HPG * tq, n_pad)
    oc_ref[:, 0] = _dot(pb, vc).reshape(NSA_HPG, tq, HEAD_DIM)
    imp4 = _dot_nt(ovt_ref[...], pb)
    imp = sum(imp4[:, hh * tq:(hh + 1) * tq] for hh in range(NSA_HPG))
    t2 = q0 + lax.broadcasted_iota(jnp.int32, (1, tq), 1)
    blk = lax.broadcasted_iota(jnp.int32, (n_blk, tq), 0)
    cur = t2 // SLC_LEN
    forced = (blk == 0) | (blk == cur) | (blk == cur - 1)
    score = jnp.where(forced, FORCE, jnp.where(blk <= cur, imp, -FORCE))

    sub = lax.broadcasted_iota(jnp.int32, (8, tq), 0)
    groups = [score[8 * v:8 * v + 8] for v in range(n_blk // 8)]
    rank = [jnp.zeros((8, tq), F32) for _ in groups]
    for i in range(n_blk):
        row = jnp.broadcast_to(score[i:i + 1], (8, tq))
        for v, grp in enumerate(groups):
            if v > i // 8:
                beats = jnp.where(row >= grp, 1.0, 0.0)
            elif v < i // 8:
                beats = jnp.where(row > grp, 1.0, 0.0)
            else:
                beats = jnp.where(sub > i % 8, jnp.where(row >= grp, 1.0, 0.0),
                                  jnp.where(row > grp, 1.0, 0.0))
            rank[v] = rank[v] + beats
    sel_bias = jnp.where(jnp.concatenate(rank, axis=0) < float(n_sel), 0.0, NEG_INF)
    selb_ref[0, 0] = sel_bias.T.astype(selb_ref.dtype)


def _nsa_cmp(qn, comp, overlap, b, s, tq):
    n_pad = comp.shape[2]
    n_blk = overlap.shape[0]
    grid = (b, NSA_GROUPS, s // tq)
    kern = functools.partial(_nsa_cmp_kernel, tq=tq, n_sel=min(TOP_N, n_blk))
    return pl.pallas_call(
        kern, grid=grid,
        in_specs=[pl.BlockSpec((NSA_HPG, 1, tq, HEAD_DIM), lambda i, g, j: (g, i, j, 0)),
                  pl.BlockSpec((1, 1, n_pad, HEAD_DIM), lambda i, g, j: (g, i, 0, 0)),
                  pl.BlockSpec((1, 1, n_pad, HEAD_DIM), lambda i, g, j: (2 + g, i, 0, 0)),
                  _const_spec((n_blk, n_pad))],
        out_specs=[pl.BlockSpec((NSA_HPG, 1, tq, HEAD_DIM), lambda i, g, j: (g, i, j, 0)),
                   pl.BlockSpec((1, 1, tq, n_blk), lambda i, g, j: (i, g, j, 0))],
        out_shape=[jax.ShapeDtypeStruct((NSA_HEADS, b, s, HEAD_DIM), F32),
                   jax.ShapeDtypeStruct((b, NSA_GROUPS, s, n_blk), BF16)],
        compiler_params=_cparams(("parallel", "parallel", "parallel")),
        name="nsa_compressed_select",
    )(qn, comp, comp, overlap)


def _head_rms(o, gain):
    return o * lax.rsqrt(jnp.mean(o * o, axis=-1, keepdims=True) + RMS_EPS) * gain


def _nsa_main_kernel(q_ref, ks_ref, kw_ref, vst_ref, vwt_ref, selb_ref, oc_ref, gate_ref, gain_ref,
                     out_ref, *, tq, tk):
    q0 = pl.program_id(2) * tq
    hq = NSA_HPG * tq
    q4 = q_ref[:, 0].reshape(hq, HEAD_DIM)
    selb = selb_ref[0, 0]
    q_sel = jnp.concatenate(
        [jnp.concatenate([q_ref[hh, 0], selb], axis=-1) for hh in range(NSA_HPG)], axis=0)
    t = q0 + lax.broadcasted_iota(jnp.int32, (1, tq), 1)

    def per_head(bias):
        return jnp.concatenate([bias] * NSA_HPG, axis=1)

    def slc_scores(j):
        start = pl.multiple_of(j * tk, tk)
        return _dot_nt(ks_ref[0, 0, pl.ds(start, tk), :], q_sel)

    def slc_tile(j, sc, carry, causal):
        m, l, acc = carry
        start = pl.multiple_of(j * tk, tk)
        if causal:
            kpos = start + lax.broadcasted_iota(jnp.int32, (tk, 1), 0)
            sc = sc + per_head(jnp.where(kpos <= t, 0.0, NEG_INF))
        m_new = jnp.maximum(m, jnp.max(sc, axis=0, keepdims=True))
        alpha = jnp.exp(m - m_new)
        p = jnp.exp(sc - m_new)
        pv = _dot(vst_ref[0, 0, :, pl.ds(start, tk)], p.astype(BF16))
        return m_new, l, alpha * acc + pv

    init = (jnp.full((1, hq), NEG_INF, F32), jnp.zeros((1, hq), F32),
            jnp.zeros((HEAD_DIM + 16, hq), F32))
    n_full = q0 // tk
    def slc_step(j, carry):
        nxt = slc_scores(j + 1)
        return (nxt,) + slc_tile(j, carry[0], carry[1:], False)

    carry = lax.fori_loop(0, n_full, slc_step, (slc_scores(0),) + init)
    _, l_s, acc_s = slc_tile(n_full, carry[0], carry[1:], True)
    o_s = (acc_s[:HEAD_DIM] / acc_s[HEAD_DIM:HEAD_DIM + 1]).T

    span = WINDOW + tq
    w0 = pl.multiple_of(jnp.maximum(q0 - WINDOW, 0), tq)
    kpos = w0 + lax.broadcasted_iota(jnp.int32, (span, 1), 0)
    wbias = jnp.where((kpos <= t) & (t - kpos < WINDOW), 0.0, NEG_INF)
    sw = _dot_nt(kw_ref[0, 0, pl.ds(w0, span), :], q4) + per_head(wbias)
    mw = jnp.max(sw, axis=0, keepdims=True)
    ew = jnp.exp(sw - mw)
    ow = _dot(vwt_ref[0, 0, :, pl.ds(w0, span)], ew.astype(BF16))
    o_w = (ow[:HEAD_DIM] / ow[HEAD_DIM:HEAD_DIM + 1]).T

    gates = gate_ref[...]
    heads = []
    for hh in range(NSA_HPG):
        rows = slice(hh * tq, (hh + 1) * tq)
        g0 = gates[:, 3 * hh:3 * hh + 1]
        g1 = gates[:, 3 * hh + 1:3 * hh + 2]
        g2 = gates[:, 3 * hh + 2:3 * hh + 3]
        o = g0 * oc_ref[hh, 0] + g1 * o_s[rows] + g2 * o_w[rows]
        heads.append(_head_rms(o, gain_ref[hh]).astype(out_ref.dtype))
    out_ref[...] = jnp.concatenate(heads, axis=-1)


def _nsa_main(qn, ks_sel, kvn, v_t, selb, oc, misc, gain16, b, s, tq, tk):
    nq = s // tq
    n_blk = selb.shape[-1]
    grid = (b, NSA_GROUPS, nq)
    vt_spec = lambda base: pl.BlockSpec((1, 1, HEAD_DIM + 16, s), lambda i, g, j: (base + g, i, 0, 0))
    kern = functools.partial(_nsa_main_kernel, tq=tq, tk=tk)
    return pl.pallas_call(
        kern, grid=grid,
        in_specs=[pl.BlockSpec((NSA_HPG, 1, tq, HEAD_DIM), lambda i, g, j: (g, i, j, 0)),
                  pl.BlockSpec((1, 1, s, HEAD_DIM + n_blk), lambda i, g, j: (g, i, 0, 0)),
                  pl.BlockSpec((1, 1, s, HEAD_DIM), lambda i, g, j: (2 + g, i, 0, 0)),
                  vt_spec(0), vt_spec(2),
                  pl.BlockSpec((1, 1, tq, n_blk), lambda i, g, j: (i, g, j, 0)),
                  pl.BlockSpec((NSA_HPG, 1, tq, HEAD_DIM), lambda i, g, j: (g, i, j, 0)),
                  pl.BlockSpec((tq, LANES), lambda i, g, j: (i * nq + j, g)),
                  pl.BlockSpec((NSA_HPG, 1, HEAD_DIM), lambda i, g, j: (g, 0, 0))],
        out_specs=pl.BlockSpec((tq, NSA_HPG * HEAD_DIM), lambda i, g, j: (i * nq + j, g)),
        out_shape=jax.ShapeDtypeStruct((b * s, NSA_HEADS * HEAD_DIM), BF16),
        compiler_params=_cparams(("parallel", "parallel", "parallel")),
        name="nsa_selected_window",
    )(qn, ks_sel, kvn, v_t, v_t, selb, oc, misc, gain16)


def _fox_kernel(q_ref, k_ref, v_ref, nc_ref, gain_ref, out_ref, o_scr, *, tq):
    i = pl.program_id(1)
    q0 = i * tq
    tk = tq
    row = q0 + lax.broadcasted_iota(jnp.int32, (tq, 1), 0)

    def head_pair(hp, _):
        hs = (2 * hp, 2 * hp + 1)
        qs = tuple(q_ref[h, 0] for h in hs)

        def tile(h, q, j, carry, causal):
            m, l, acc = carry
            start = pl.multiple_of(j * tk, tk)
            k = k_ref[h, 0, pl.ds(start, tk), :]
            v = v_ref[h, 0, pl.ds(start, tk), :]
            sc = _dot_nt(q, k) + nc_ref[0, h, j]
            if causal:
                kpos = start + lax.broadcasted_iota(jnp.int32, (1, tk), 1)
                sc = jnp.where(kpos <= row, sc, NEG_INF)
            m_new = jnp.maximum(m, jnp.max(sc, axis=-1, keepdims=True))
            alpha = jnp.exp(m - m_new)
            p = jnp.exp(sc - m_new)
            l = alpha * l + jnp.sum(p, axis=-1, keepdims=True)
            return m_new, l, alpha * acc + _dot(p.astype(BF16), v)

        def both(j, carries, causal):
            return tuple(tile(h, q, j, c, causal) for h, q, c in zip(hs, qs, carries))

        init = (jnp.full((tq, 1), NEG_INF, F32), jnp.zeros((tq, 1), F32),
                jnp.zeros((tq, HEAD_DIM), F32))
        carries = lax.fori_loop(0, i, lambda j, c: both(j, c, False), (init, init))
        for h, (_, l, acc) in zip(hs, both(i, carries, True)):
            o_scr[h] = _head_rms(acc / l, gain_ref[h])
        return 0

    lax.fori_loop(0, FOX_HEADS // 2, head_pair, 0)
    out_ref[...] = jnp.concatenate([o_scr[h] for h in range(FOX_HEADS)],
                                   axis=-1).astype(out_ref.dtype)


def _fox(fox, negc, gain16, b, s, tq):
    nq = s // tq
    kern = functools.partial(_fox_kernel, tq=tq)
    return pl.pallas_call(
        kern, grid=(b, nq),
        in_specs=[pl.BlockSpec((FOX_HEADS, 1, tq, HEAD_DIM), lambda i, j: (0, i, j, 0)),
                  pl.BlockSpec((FOX_HEADS, 1, s, HEAD_DIM), lambda i, j: (1, i, 0, 0)),
                  pl.BlockSpec((FOX_HEADS, 1, s, HEAD_DIM), lambda i, j: (2, i, 0, 0)),
                  pl.BlockSpec((1, FOX_HEADS, nq, 1, tq), lambda i, j: (i, 0, 0, 0, 0)),
                  pl.BlockSpec((FOX_HEADS, 1, HEAD_DIM), lambda i, j: (2, 0, 0))],
        out_specs=pl.BlockSpec((tq, FOX_HEADS * HEAD_DIM), lambda i, j: (i * nq + j, 0)),
        out_shape=jax.ShapeDtypeStruct((b * s, FOX_HEADS * HEAD_DIM), BF16),
        scratch_shapes=[pltpu.VMEM((FOX_HEADS, tq, HEAD_DIM), F32)],
        compiler_params=_cparams(("parallel", "parallel")),
        name="fox_attention",
    )(fox, fox, fox, negc, gain16)


def _sb_kernel(q_ref, k_ref, v_ref, gain_ref, out_ref, acc_ref, *, tq, tk):
    i = pl.program_id(1)
    q0 = i * tq
    row = q0 + lax.broadcasted_iota(jnp.int32, (tq, 1), 0)
    r_i = lax.broadcasted_iota(jnp.int32, (tk, tk), 0)
    c_i = lax.broadcasted_iota(jnp.int32, (tk, tk), 1)
    suffix = jnp.where(r_i >= c_i, 1.0, 0.0).astype(BF16)
    assert tq == tk
    heads = range(SB_HEADS)
    qs = [q_ref[h, 0] for h in heads]
    acc_ref[...] = jnp.zeros_like(acc_ref)

    def scores(h, j):
        start = pl.multiple_of(jnp.maximum(j, 0) * tk, tk)
        return _dot_nt(qs[h], k_ref[h, 0, pl.ds(start, tk), :])

    def tile(h, j, z, tail, masked):
        start = pl.multiple_of(j * tk, tk)
        v = v_ref[h, 0, pl.ds(start, tk), :]
        lr = -(jnp.maximum(z, 0.0) + jnp.log(1.0 + jnp.exp(-jnp.abs(z))))
        if masked:
            mask = (start + lax.broadcasted_iota(jnp.int32, (1, tk), 1)) < row
            lr = jnp.where(mask, lr, 0.0)
        hi = lr.astype(BF16)
        lo = (lr - hi.astype(F32)).astype(BF16)
        both = _dot(jnp.concatenate([hi, lo], axis=0), suffix)
        cs = both[:tq] + both[tq:]
        a = jnp.exp(z + cs + tail)
        if masked:
            a = jnp.where(mask, a, 0.0)
        acc_ref[h] += _dot(a.astype(BF16), v)
        return tail + cs[:, 0:1]

    zs = tuple(scores(h, i) for h in heads)
    nxt = tuple(scores(h, i - 1) for h in heads)
    tails = tuple(tile(h, i, zs[h], jnp.zeros((tq, 1), F32), True) for h in heads)

    def step(jj, carry):
        zs, tails = carry
        j = i - 1 - jj
        nxt = tuple(scores(h, j - 1) for h in heads)
        return nxt, tuple(tile(h, j, zs[h], tails[h], False) for h in heads)

    lax.fori_loop(0, i, step, (nxt, tails))
    out_ref[...] = jnp.concatenate([_head_rms(acc_ref[h], gain_ref[h]) for h in heads],
                                   axis=-1).astype(out_ref.dtype)


def _stick_breaking(sb, gain16, b, s, tq, tk):
    nq = s // tq
    kern = functools.partial(_sb_kernel, tq=tq, tk=tk)
    return pl.pallas_call(
        kern, grid=(b, nq),
        in_specs=[pl.BlockSpec((SB_HEADS, 1, tq, HEAD_DIM), lambda i, j: (0, i, j, 0)),
                  pl.BlockSpec((SB_HEADS, 1, s, HEAD_DIM), lambda i, j: (1, i, 0, 0)),
                  pl.BlockSpec((SB_HEADS, 1, s, HEAD_DIM), lambda i, j: (2, i, 0, 0)),
                  pl.BlockSpec((SB_HEADS, 1, HEAD_DIM), lambda i, j: (3, 0, 0))],
        out_specs=pl.BlockSpec((tq, SB_HEADS * HEAD_DIM), lambda i, j: (i * nq + j, 0)),
        out_shape=jax.ShapeDtypeStruct((b * s, SB_HEADS * HEAD_DIM), BF16),
        scratch_shapes=[pltpu.VMEM((SB_HEADS, tq, HEAD_DIM), F32)],
        compiler_params=_cparams(("parallel", "parallel")),
        name="stick_breaking_attention",
    )(sb, sb, sb, gain16)


def _rms_rows(x, gain):
    return x * lax.rsqrt(jnp.mean(x * x, axis=-1, keepdims=True) + RMS_EPS) * gain


def _residual_kernel(h_ref, on_ref, of_ref, os_ref, p_ref, wo_ref, gf_ref, wg_ref, wu_ref, wd_ref,
                     gp_ref, wp_ref, wpg_ref, gfin_ref, out_ref, *, tf, final):
    n1 = on_ref.shape[1]
    n2 = n1 + of_ref.shape[1]
    mix = (_dot(on_ref[...], wo_ref[:n1, :]) + _dot(of_ref[...], wo_ref[n1:n2, :])
           + _dot(os_ref[...], wo_ref[n2:, :]))
    h = h_ref[...] + mix
    hn = _rms_rows(h, gf_ref[...]).astype(BF16)
    acc = h
    for c in range(wg_ref.shape[1] // tf):
        sl = slice(c * tf, (c + 1) * tf)
        gate = _dot(hn, wg_ref[:, sl])
        up = _dot(hn, wu_ref[:, sl])
        act = (gate * jax.nn.sigmoid(gate) * up).astype(BF16)
        acc = acc + _dot(act, wd_ref[sl, :])
    hn = _rms_rows(acc, gp_ref[...]).astype(BF16)
    gate = jax.nn.sigmoid(_dot(hn, wpg_ref[...]))
    new = acc + _dot(p_ref[...].astype(BF16), wp_ref[...]) * gate
    if final:
        new = _rms_rows(new, gfin_ref[...])
    out_ref[...] = new


def _residual_updates(h2d, o_nsa, o_fox, o_sb, p2d, w_out, g_ffn, wg, wu, wd, g_ple, wp, wpg,
                      g_final, final, tm, tf):
    t, d = h2d.shape
    row = lambda n: pl.BlockSpec((tm, n), lambda i: (i, 0))
    vec = _const_spec((1, d))
    return pl.pallas_call(
        functools.partial(_residual_kernel, tf=tf, final=final), grid=(t // tm,),
        in_specs=[row(d), row(o_nsa.shape[1]), row(o_fox.shape[1]), row(o_sb.shape[1]),
                  row(p2d.shape[1]), _const_spec(w_out.shape), vec, _const_spec(wg.shape),
                  _const_spec(wu.shape), _const_spec(wd.shape), vec, _const_spec(wp.shape),
                  _const_spec(wpg.shape), vec],
        out_specs=row(d),
        out_shape=jax.ShapeDtypeStruct((t, d), F32),
        compiler_params=_cparams(("parallel",)),
        name="residual_updates",
    )(h2d, o_nsa, o_fox, o_sb, p2d, w_out, g_ffn, wg, wu, wd, g_ple, wp, wpg, g_final)


def _overlap_matrix(n_pad, n_blk):
    n_cmp = n_pad - 1
    cmp_start = np.arange(n_pad) * CMP_STRIDE
    blk_start = np.arange(n_blk) * SLC_LEN
    ov = ((cmp_start[:, None] < blk_start[None, :] + SLC_LEN)
          & (cmp_start[:, None] + CMP_LEN > blk_start[None, :])
          & (np.arange(n_pad)[:, None] < n_cmp))
    return jnp.asarray(ov.T.astype(np.float32), dtype=BF16)


def _pick(n, pref):
    return pref if n % pref == 0 else n


def kernel(x, p, positions, norm_mix, w_in, b_nsa_gate, b_forget, nsa_cmp_pos_k, nsa_cmp_w1_k, nsa_cmp_w2_k, nsa_cmp_pos_v, nsa_cmp_w1_v, nsa_cmp_w2_v, head_norm, w_out, norm_ffn, w_ffn_gate, w_ffn_up, w_ffn_down, norm_ple, w_ple_proj, w_ple_gate, norm_final):
    b, s, d = x.shape
    depth = w_in.shape[0]
    t = b * s
    assert s % 512 == 0 and s >= WINDOW + 128
    tm = _pick(t, 512)
    tq_nsa, tk_nsa = 256, 256
    tq_cmp = 1024
    tq_fox = 512
    tq_sb, tk_sb = 256, 256
    n_chunk = s // CMP_STRIDE
    n_blk = s // SLC_LEN

    half = HEAD_DIM // 2
    inv_freq = ROPE_THETA ** (-jnp.arange(half, dtype=F32) / half)
    ang = positions.astype(F32)[..., None] * inv_freq
    cos, sin = jnp.cos(ang), jnp.sin(ang)
    cos_t = jnp.concatenate([cos] * 4, axis=-1).reshape(t, LANES)
    sin_t = jnp.concatenate([-sin, sin, -sin, sin], axis=-1).reshape(t, LANES)
    overlap = _overlap_matrix(n_chunk, n_blk)
    blk_onehot = jnp.asarray(
        (np.arange(s)[:, None] // SLC_LEN == np.arange(n_blk)[None, :]).astype(np.float32), dtype=BF16)

    h = x.reshape(t, d)
    for i in range(depth):
        w_perm = _permute_w_in(w_in[i]).astype(BF16)
        bias = _misc_bias(b_nsa_gate[i], b_forget[i])
        qn, kvn, cmp4, fox, sb, misc = _in_projection(
            h, norm_mix[i].reshape(1, d), w_perm, cos_t, sin_t, bias, tm)

        width = CMP_STRIDE * HEAD_DIM
        pos = jnp.stack([nsa_cmp_pos_k[i], nsa_cmp_pos_v[i]]).reshape(2, 2, width)
        w1 = jnp.stack([nsa_cmp_w1_k[i], nsa_cmp_w1_v[i]]).astype(BF16)
        w2 = jnp.stack([nsa_cmp_w2_k[i], nsa_cmp_w2_v[i]]).astype(BF16)
        comp = _compress(cmp4, pos, w1, w2, b, s)

        gain16 = head_norm[i].reshape(N_HEADS, 1, HEAD_DIM)
        qn4 = qn.reshape(NSA_HEADS, b, s, HEAD_DIM)
        oc, selb = _nsa_cmp(qn4, comp, overlap, b, s, tq_cmp)
        kvn4 = kvn.reshape(8, b, s, HEAD_DIM)
        ks_sel = jnp.concatenate(
            [kvn4[:NSA_GROUPS], jnp.broadcast_to(blk_onehot, (NSA_GROUPS, b, s, n_blk))], axis=-1)
        v_t = jnp.concatenate([jnp.swapaxes(kvn4[4:], 2, 3), jnp.ones((4, b, 16, s), BF16)], axis=2)
        o_nsa = _nsa_main(qn4, ks_sel, kvn4, v_t, selb, oc, misc, gain16, b, s, tq_nsa, tk_nsa)

        log_f = misc[:, 12:16].reshape(b, s // LANES, LANES, FOX_HEADS).transpose(0, 3, 1, 2)
        negc = _neg_cumsum(log_f).reshape(b, FOX_HEADS, s // tq_fox, 1, tq_fox)
        o_fox = _fox(fox.reshape(12, b, s, HEAD_DIM), negc, gain16, b, s, tq_fox)
        o_sb = _stick_breaking(sb.reshape(12, b, s, HEAD_DIM), gain16, b, s, tq_sb, tk_sb)

        h = _residual_updates(
            h, o_nsa, o_fox, o_sb, p[i].reshape(t, -1), w_out[i].astype(BF16),
            norm_ffn[i].reshape(1, d), w_ffn_gate[i].astype(BF16), w_ffn_up[i].astype(BF16),
            w_ffn_down[i].astype(BF16), norm_ple[i].reshape(1, d), w_ple_proj[i].astype(BF16),
            w_ple_gate[i].astype(BF16), norm_final.reshape(1, d), i == depth - 1, tm, 256)
    return h.reshape(b, s, d)
```
